```python
import math
import jax, jax.numpy as jnp
from jax import lax
import numpy as np

D_MODEL = 1024
BATCH = 32
SEQ = 256
DEPTH = 1
DEC_BATCH = 2
DEC_SEQ = 4096
PAST_LEN = 512

GRID_W = 64
CHUNK = 128
A_GROUPS = 4
A_GROUP_DIM = 128
A_WIDTH = A_GROUPS * A_GROUP_DIM
MLA_HEADS = 4
QK_NOPE = 128
QK_ROPE = 64
ROPE_DIM = QK_ROPE
V_DIM = 128
Q_RANK = 384
KV_RANK = 256
B_WIDTH = MLA_HEADS * V_DIM
Q_BLOCK = 128
ROPE_THETA = 10000.0
IN_COLS = 2 * A_WIDTH + Q_RANK + KV_RANK + ROPE_DIM
SPLITS = (A_WIDTH, 2 * A_WIDTH, 2 * A_WIDTH + Q_RANK, 2 * A_WIDTH + Q_RANK + KV_RANK)
MIX_WIDTH = A_WIDTH + B_WIDTH
N_KEYS = 128
N_EXPERTS = N_KEYS * N_KEYS
PEER_HEADS = 8
PEER_HALF = 128
PEER_QDIM = 2 * PEER_HALF
PEER_TOPK = 16
PEER_BLOCK = 128
ALPHA = (2.0 * DEPTH) ** 0.25
BETA = (8.0 * DEPTH) ** -0.25
EPS = 1e-6

kernel_name = "hybrid_diffusion_chunkmlp_mla_peer_step"


def layer_norm(x, g, b):
    xf = x.astype(jnp.float32)
    mu = jnp.mean(xf, -1, keepdims=True)
    var = jnp.mean(jnp.square(xf - mu), -1, keepdims=True)
    y = (xf - mu) * lax.rsqrt(var + EPS)
    return (y * g.astype(jnp.float32) + b.astype(jnp.float32)).astype(x.dtype)


def standardize(x):
    xf = x.astype(jnp.float32)
    mu = jnp.mean(xf, -1, keepdims=True)
    var = jnp.mean(jnp.square(xf - mu), -1, keepdims=True)
    return ((xf - mu) * lax.rsqrt(var + EPS)).astype(x.dtype)


def rms_norm(x, g):
    xf = x.astype(jnp.float32)
    y = xf * lax.rsqrt(jnp.mean(jnp.square(xf), -1, keepdims=True) + EPS)
    return (y * g.astype(jnp.float32)).astype(x.dtype)


def rope_1d(x, pos):
    f = x.shape[-1] // 2
    freqs = 1.0 / (ROPE_THETA ** (jnp.arange(f, dtype=jnp.float32) / f))
    ang = pos[:, None] * freqs[None, :]
    ang = ang.reshape((ang.shape[0],) + (1,) * (x.ndim - 3) + (f,))
    cos, sin = jnp.cos(ang).astype(x.dtype), jnp.sin(ang).astype(x.dtype)
    x1, x2 = x[..., :f], x[..., f:]
    return jnp.concatenate([x1 * cos - x2 * sin, x1 * sin + x2 * cos], axis=-1)


def rope_2d(x, row, col):
    half = x.shape[-1] // 2
    return jnp.concatenate([rope_1d(x[..., :half], row), rope_1d(x[..., half:], col)], axis=-1)


def chunk_sgu(u, v, w_s, b_s):
    B, N, _ = u.shape
    v = standardize(v.reshape(B, N // CHUNK, CHUNK, A_GROUPS, A_GROUP_DIM))
    mixed = jnp.einsum('gpq,bcqgd->bcpgd', w_s, v) + b_s.T[:, :, None]
    return u * mixed.reshape(B, N, A_WIDTH)


def mla_attention(q_nope, q_rope, k_nope, k_rope, v):
    B, N, H, _ = q_nope.shape
    nb = N // Q_BLOCK
    scale = 1.0 / math.sqrt(QK_NOPE + QK_ROPE)
    qn = jnp.moveaxis(q_nope.reshape(B, nb, Q_BLOCK, H, QK_NOPE), 1, 0)
    qr = jnp.moveaxis(q_rope.reshape(B, nb, Q_BLOCK, H, QK_ROPE), 1, 0)

    def one_block(blk):
        bn, br = blk
        s = (jnp.einsum('bqhd,bkhd->bhqk', bn, k_nope)
             + jnp.einsum('bqhr,bkr->bhqk', br, k_rope))
        p = jax.nn.softmax(s.astype(jnp.float32) * scale, axis=-1).astype(v.dtype)
        return jnp.einsum('bhqk,bkhd->bqhd', p, v)

    o = lax.map(one_block, (qn, qr))
    return jnp.moveaxis(o, 0, 1).reshape(B, N, H * V_DIM)


def peer(h, w_pq, sub_keys, u_tab, v_tab):
    B, N, D = h.shape
    T = B * N
    hf = h.reshape(T, D)
    q = (hf @ w_pq).reshape(T, PEER_HEADS, 2, PEER_HALF)
    s = jnp.einsum('thsd,hskd->thsk', q, sub_keys)
    s_top, i_top = lax.top_k(s, PEER_TOPK)
    cand_s = (s_top[:, :, 0, :, None] + s_top[:, :, 1, None, :]).reshape(T, PEER_HEADS, PEER_TOPK * PEER_TOPK)
    cand_i = (i_top[:, :, 0, :, None] * N_KEYS + i_top[:, :, 1, None, :]).reshape(T, PEER_HEADS, PEER_TOPK * PEER_TOPK)
    f_s, f_pos = lax.top_k(cand_s, PEER_TOPK)
    ids = jnp.take_along_axis(cand_i, f_pos, axis=-1)
    g = jax.nn.softmax(f_s.astype(jnp.float32), axis=-1).astype(h.dtype)
    nb = T // PEER_BLOCK
    hb = hf.reshape(nb, PEER_BLOCK, D)
    ib = ids.reshape(nb, PEER_BLOCK, PEER_HEADS * PEER_TOPK)
    gb = g.reshape(nb, PEER_BLOCK, PEER_HEADS * PEER_TOPK)

    def one_block(blk):
        xb, eb, wb = blk
        u = jnp.take(u_tab, eb, axis=0)
        a = jax.nn.gelu(jnp.einsum('tkd,td->tk', u, xb), approximate=False)
        vv = jnp.take(v_tab, eb, axis=0)
        return jnp.einsum('tk,tkd->td', wb * a, vv)

    out = lax.map(one_block, (hb, ib, gb))
    return out.reshape(B, N, D)


def trunk_layer(x, mod, ctx_ckv, ctx_krope, is_latent, w_in, w_s, b_s, g_q, w_qb, g_kv, w_kvb,
                w_o, ln1_g, ln1_b, w_pq, sub_keys, u_tab, v_tab, ln2_g, ln2_b):
    B, N, _ = x.shape
    shift1, scale1, gate1, shift2, scale2, gate2 = [mod[:, i][:, None, :] for i in range(6)]
    h = x * (1.0 + scale1) + shift1
    z = h @ w_in
    ua, va, q_lat, kv_lat, k_rope = jnp.split(z, SPLITS, axis=-1)
    a_out = chunk_sgu(jax.nn.gelu(ua, approximate=False), jax.nn.gelu(va, approximate=False), w_s, b_s)
    q = (rms_norm(q_lat, g_q) @ w_qb).reshape(B, N, MLA_HEADS, QK_NOPE + QK_ROPE)
    q_nope, q_rope = q[..., :QK_NOPE], q[..., QK_NOPE:]
    ckv = rms_norm(kv_lat, g_kv)
    if is_latent:
        rows = N // GRID_W
        row = jnp.repeat(jnp.arange(rows, dtype=jnp.float32), GRID_W)
        col = jnp.tile(jnp.arange(GRID_W, dtype=jnp.float32), rows)
        q_rope = rope_2d(q_rope, row, col)
        k_rope = rope_2d(k_rope, row, col)
        keys_ckv = jnp.concatenate([ctx_ckv, ckv], axis=1)
        keys_kr = jnp.concatenate([ctx_krope, k_rope], axis=1)
    else:
        keys_ckv, keys_kr = ckv, k_rope
    M = keys_ckv.shape[1]
    kv = (keys_ckv @ w_kvb).reshape(B, M, MLA_HEADS, QK_NOPE + V_DIM)
    k_nope, v = kv[..., :QK_NOPE], kv[..., QK_NOPE:]
    b_out = mla_attention(q_nope, q_rope, k_nope, keys_kr, v)
    mix = jnp.concatenate([a_out, b_out], axis=-1) @ w_o
    x = layer_norm(ALPHA * x + gate1 * mix, ln1_g, ln1_b)
    h2 = x * (1.0 + scale2) + shift2
    x = layer_norm(ALPHA * x + gate2 * peer(h2, w_pq, sub_keys, u_tab, v_tab), ln2_g, ln2_b)
    return x, ckv, k_rope


def setup_inputs(seed: int = 0) -> dict:
    key = jax.random.key(seed)
    ks = jax.random.split(key, 32)
    f32 = jnp.float32
    nrm = lambda k, shape, s: jax.random.normal(k, shape, f32) * s
    L, D = DEPTH, D_MODEL
    return {
        "x_prompt": nrm(ks[0], (BATCH, SEQ, D), 1.0),
        "x_sample": nrm(ks[1], (DEC_BATCH, DEC_SEQ, D), 1.0),
        "cache_ckv": nrm(ks[2], (DEC_BATCH, L, PAST_LEN, KV_RANK), 1.0),
        "cache_krope": nrm(ks[3], (DEC_BATCH, L, PAST_LEN, ROPE_DIM), 1.0),
        "c": nrm(ks[4], (DEC_BATCH, D), 1.0),
        "c_ctx": nrm(ks[5], (D,), 1.0),
        "w_mod": nrm(ks[6], (L, D, 6 * D), D ** -0.5),
        "b_mod": nrm(ks[7], (L, 6 * D), 0.01),
        "w_in": nrm(ks[8], (L, D, IN_COLS), D ** -0.5),
        "w_s": nrm(ks[9], (L, A_GROUPS, CHUNK, CHUNK), CHUNK ** -0.5),
        "b_s": 1.0 + nrm(ks[10], (L, A_GROUPS, CHUNK), 0.01),
        "g_q": 1.0 + nrm(ks[11], (L, Q_RANK), 0.01),
        "w_qb": nrm(ks[12], (L, Q_RANK, MLA_HEADS * (QK_NOPE + QK_ROPE)), Q_RANK ** -0.5),
        "g_kv": 1.0 + nrm(ks[13], (L, KV_RANK), 0.01),
        "w_kvb": nrm(ks[14], (L, KV_RANK, MLA_HEADS * (QK_NOPE + V_DIM)), KV_RANK ** -0.5),
        "w_o": nrm(ks[15], (L, MIX_WIDTH, D), BETA * MIX_WIDTH ** -0.5),
        "ln1_g": 1.0 + nrm(ks[16], (L, D), 0.01),
        "ln1_b": nrm(ks[17], (L, D), 0.01),
        "w_pq": nrm(ks[18], (L, D, PEER_HEADS * PEER_QDIM), D ** -0.5),
        "sub_keys": nrm(ks[19], (L, PEER_HEADS, 2, N_KEYS, PEER_HALF), PEER_HALF ** -0.5),
        "u_tab": nrm(ks[20], (L, N_EXPERTS, D), D ** -0.5),
        "v_tab": nrm(ks[21], (L, N_EXPERTS, D), BETA),
        "ln2_g": 1.0 + nrm(ks[22], (L, D), 0.01),
        "ln2_b": nrm(ks[23], (L, D), 0.01),
    }


def reference(x_prompt, x_sample, cache_ckv, cache_krope, c, c_ctx, w_mod, b_mod, w_in, w_s, b_s,
              g_q, w_qb, g_kv, w_kvb, w_o, ln1_g, ln1_b, w_pq, sub_keys, u_tab, v_tab, ln2_g, ln2_b):
    y_p = x_prompt
    new_ckv_list, new_krope_list = [], []
    for l in range(DEPTH):
        mod_ctx = (jax.nn.silu(c_ctx)[None, :] @ w_mod[l] + b_mod[l]).reshape(1, 6, D_MODEL)
        y_p, ckv_l, kr_l = trunk_layer(
            y_p, mod_ctx, None, None, False, w_in[l], w_s[l], b_s[l], g_q[l], w_qb[l], g_kv[l],
            w_kvb[l], w_o[l], ln1_g[l], ln1_b[l], w_pq[l], sub_keys[l], u_tab[l], v_tab[l],
            ln2_g[l], ln2_b[l])
        new_ckv_list.append(ckv_l)
        new_krope_list.append(kr_l)
    new_ckv = jnp.stack(new_ckv_list, axis=1)
    new_krope = jnp.stack(new_krope_list, axis=1)

    y_s = x_sample
    for l in range(DEPTH):
        mod_lat = (jax.nn.silu(c) @ w_mod[l] + b_mod[l]).reshape(c.shape[0], 6, D_MODEL)
        y_s, _, _ = trunk_layer(
            y_s, mod_lat, cache_ckv[:, l], cache_krope[:, l], True, w_in[l], w_s[l], b_s[l], g_q[l],
            w_qb[l], g_kv[l], w_kvb[l], w_o[l], ln1_g[l], ln1_b[l], w_pq[l], sub_keys[l], u_tab[l],
            v_tab[l], ln2_g[l], ln2_b[l])
    return (y_p, y_s, new_ckv, new_krope)
```

```python
import functools
import math

import jax
import jax.numpy as jnp
import numpy as np
from jax import lax
from jax.experimental import pallas as pl
from jax.experimental.pallas import tpu as pltpu

F32 = jnp.float32
BF16 = jnp.bfloat16

D_MODEL = 1024
GRID_W = 64
CHUNK = 128
A_GROUPS = 4
A_WIDTH = 512
MLA_HEADS = 4
QK_NOPE = 128
QK_ROPE = 64
V_DIM = 128
Q_RANK = 384
KV_RANK = 256
ROPE_THETA = 10000.0
N_KEYS = 128
PEER_HEADS = 8
PEER_TOPK = 16
N_EXPERTS = N_KEYS * N_KEYS
DEPTH = 1
ALPHA = (2.0 * DEPTH) ** 0.25
EPS = 1e-6

LANES = 128
HEAD_W = 2 * LANES
IN_EXT = 2 * A_WIDTH + Q_RANK + KV_RANK + 2 * QK_ROPE

PRE_TT = 256
ATT_TQ = 256
POST_TT = 256
PEER_TT = 512
PEER_ET = 1024
VMEM_LIMIT = 56 * 1024 * 1024


def _cparams(sem):
    return pltpu.CompilerParams(dimension_semantics=sem, vmem_limit_bytes=VMEM_LIMIT)


def _gelu(x):
    return x * (lax.erf(x * (1.0 / math.sqrt(2.0))) + 1.0) * 0.5


def _dot(a, b):
    return jnp.dot(a, b, preferred_element_type=F32)


def _mod_kernel(c_ref, w_ref, b_ref, o_ref):
    c = c_ref[...]
    sc = c * (1.0 / (1.0 + jnp.exp(-c)))
    o_ref[...] = _dot(sc.astype(BF16), w_ref[...].astype(BF16)) + b_ref[...]


def _mod_call(c_rows, w_mod, b_mod):
    n = w_mod.shape[1]
    bn = 1536
    return pl.pallas_call(
        _mod_kernel,
        grid=(n // bn,),
        in_specs=[pl.BlockSpec((8, D_MODEL), lambda j: (0, 0)),
                  pl.BlockSpec((D_MODEL, bn), lambda j: (0, j)),
                  pl.BlockSpec((1, bn), lambda j: (0, j))],
        out_specs=pl.BlockSpec((8, bn), lambda j: (0, j)),
        out_shape=jax.ShapeDtypeStruct((8, n), F32),
        compiler_params=_cparams(("arbitrary",)),
        name="mod",
    )(c_rows, w_mod, b_mod)


def _rms(x, g):
    return x * lax.rsqrt(jnp.mean(x * x, axis=-1, keepdims=True) + EPS) * g


def _pre_kernel(x_ref, mod_ref, win_ref, ws_ref, bsb_ref, gq_ref, wqb_ref, gkv_ref, wkvb_ref,
                cos_ref, sin_ref, a_ref, q_ref, k_ref, v_ref, ckv_ref, kr_ref):
    x = x_ref[...]
    shift1 = mod_ref[0, 0:1, :]
    scale1 = mod_ref[0, 1:2, :]
    h = x * (1.0 + scale1) + shift1
    z = _dot(h.astype(BF16), win_ref[...])
    tt = x.shape[0]

    for g in range(A_GROUPS):
        lo = g * LANES
        ug = _gelu(z[:, lo:lo + LANES])
        vg = _gelu(z[:, A_WIDTH + lo:A_WIDTH + lo + LANES])
        mu = jnp.mean(vg, axis=-1, keepdims=True)
        vc = vg - mu
        var = jnp.mean(vc * vc, axis=-1, keepdims=True)
        vs = (vc * lax.rsqrt(var + EPS)).astype(BF16)
        for c in range(tt // CHUNK):
            r0 = c * CHUNK
            mixed = _dot(ws_ref[g], vs[r0:r0 + CHUNK, :]) + bsb_ref[g]
            a_ref[r0:r0 + CHUNK, lo:lo + LANES] = (ug[r0:r0 + CHUNK, :] * mixed).astype(BF16)

    cos = cos_ref[...]
    sin = sin_ref[...]
    o_q = 2 * A_WIDTH
    qn = _rms(z[:, o_q:o_q + Q_RANK], gq_ref[...])
    zq = _dot(qn.astype(BF16), wqb_ref[...])
    for hd in range(MLA_HEADS):
        c0 = hd * HEAD_W
        q_ref[:, c0:c0 + LANES] = zq[:, c0:c0 + LANES].astype(BF16)
        blk = zq[:, c0 + LANES:c0 + HEAD_W]
        q_ref[:, c0 + LANES:c0 + HEAD_W] = (blk * cos + pltpu.roll(blk, 64, 1) * sin).astype(BF16)

    o_kv = o_q + Q_RANK
    ckv = _rms(z[:, o_kv:o_kv + KV_RANK], gkv_ref[...])
    ckv_ref[...] = ckv
    o_kr = o_kv + KV_RANK
    kr_ref[...] = z[:, o_kr:o_kr + QK_ROPE]
    kv = _dot(ckv.astype(BF16), wkvb_ref[...])
    kblk = z[:, o_kr:o_kr + LANES]
    yk = (kblk * cos + pltpu.roll(kblk, 64, 1) * sin).astype(BF16)
    for hd in range(MLA_HEADS):
        c0 = hd * HEAD_W
        k_ref[:, c0:c0 + LANES] = kv[:, c0:c0 + LANES].astype(BF16)
        k_ref[:, c0 + LANES:c0 + HEAD_W] = yk
        v_ref[:, hd * V_DIM:(hd + 1) * V_DIM] = kv[:, c0 + LANES:c0 + HEAD_W].astype(BF16)


def _pre_call(x, mod, mod_row_of_tile, pos_tiles, w, cos_t, sin_t):
    t = x.shape[0]
    tt = PRE_TT
    full = lambda shape: pl.BlockSpec(shape, lambda i: (0,) * len(shape))
    tok = lambda width: pl.BlockSpec((tt, width), lambda i: (i, 0))
    return pl.pallas_call(
        _pre_kernel,
        grid=(t // tt,),
        in_specs=[tok(D_MODEL),
                  pl.BlockSpec((1, 6, D_MODEL), lambda i: (mod_row_of_tile(i), 0, 0)),
                  full((D_MODEL, IN_EXT)), full((A_GROUPS, CHUNK, CHUNK)), full((A_GROUPS, CHUNK, LANES)),
                  full((1, Q_RANK)), full((Q_RANK, MLA_HEADS * HEAD_W)),
                  full((1, KV_RANK)), full((KV_RANK, MLA_HEADS * HEAD_W)),
                  pl.BlockSpec((tt, LANES), lambda i: (i % pos_tiles, 0)),
                  pl.BlockSpec((tt, LANES), lambda i: (i % pos_tiles, 0))],
        out_specs=[tok(A_WIDTH), tok(MLA_HEADS * HEAD_W), tok(MLA_HEADS * HEAD_W), tok(MLA_HEADS * V_DIM),
                   tok(KV_RANK), tok(QK_ROPE)],
        out_shape=[jax.ShapeDtypeStruct((t, A_WIDTH), BF16),
                   jax.ShapeDtypeStruct((t, MLA_HEADS * HEAD_W), BF16),
                   jax.ShapeDtypeStruct((t, MLA_HEADS * HEAD_W), BF16),
                   jax.ShapeDtypeStruct((t, MLA_HEADS * V_DIM), BF16),
                   jax.ShapeDtypeStruct((t, KV_RANK), F32),
                   jax.ShapeDtypeStruct((t, QK_ROPE), F32)],
        compiler_params=_cparams(("parallel",)),
        name="pre",
    )(x, mod, w["w_in"], w["w_s"], w["bsb"], w["g_q"], w["w_qb"], w["g_kv"], w["w_kvb"], cos_t, sin_t)


def _cache_kernel(ckv_ref, krp_ref, wkvb_ref, k_ref, v_ref):
    kv = _dot(ckv_ref[...].astype(BF16), wkvb_ref[...])
    yk = krp_ref[...].astype(BF16)
    for hd in range(MLA_HEADS):
        c0 = hd * HEAD_W
        k_ref[:, c0:c0 + LANES] = kv[:, c0:c0 + LANES].astype(BF16)
        k_ref[:, c0 + LANES:c0 + HEAD_W] = yk
        v_ref[:, hd * V_DIM:(hd + 1) * V_DIM] = kv[:, c0 + LANES:c0 + HEAD_W].astype(BF16)


def _cache_call(ckv, krp, w_kvb):
    t = ckv.shape[0]
    tt = 256
    return pl.pallas_call(
        _cache_kernel,
        grid=(t // tt,),
        in_specs=[pl.BlockSpec((tt, KV_RANK), lambda i: (i, 0)),
                  pl.BlockSpec((tt, LANES), lambda i: (i, 0)),
                  pl.BlockSpec((KV_RANK, MLA_HEADS * HEAD_W), lambda i: (0, 0))],
        out_specs=[pl.BlockSpec((tt, MLA_HEADS * HEAD_W), lambda i: (i, 0)),
                   pl.BlockSpec((tt, MLA_HEADS * V_DIM), lambda i: (i, 0))],
        out_shape=[jax.ShapeDtypeStruct((t, MLA_HEADS * HEAD_W), BF16),
                   jax.ShapeDtypeStruct((t, MLA_HEADS * V_DIM), BF16)],
        compiler_params=_cparams(("parallel",)),
        name="cache_kv",
    )(ckv, krp, w_kvb)


def _attn_kernel(q_ref, k_ref, v_ref, o_ref):
    scale = 1.0 / math.sqrt(QK_NOPE + QK_ROPE)
    for hd in range(MLA_HEADS):
        c0 = hd * HEAD_W
        q = q_ref[0, :, c0:c0 + HEAD_W]
        k = k_ref[0, :, c0:c0 + HEAD_W]
        s = lax.dot_general(q, k, (((1,), (1,)), ((), ())), preferred_element_type=F32) * scale
        m = jnp.max(s, axis=-1, keepdims=True)
        p = jnp.exp(s - m)
        l = jnp.sum(p, axis=-1, keepdims=True)
        o = _dot(p.astype(BF16), v_ref[0, :, hd * V_DIM:(hd + 1) * V_DIM])
        o_ref[0, :, hd * V_DIM:(hd + 1) * V_DIM] = (o / l).astype(BF16)


def _attn_call(q, k, v):
    b, n, _ = q.shape
    m = k.shape[1]
    tq = ATT_TQ
    return pl.pallas_call(
        _attn_kernel,
        grid=(b, n // tq),
        in_specs=[pl.BlockSpec((1, tq, MLA_HEADS * HEAD_W), lambda i, j: (i, j, 0)),
                  pl.BlockSpec((1, m, MLA_HEADS * HEAD_W), lambda i, j: (i, 0, 0)),
                  pl.BlockSpec((1, m, MLA_HEADS * V_DIM), lambda i, j: (i, 0, 0))],
        out_specs=pl.BlockSpec((1, tq, MLA_HEADS * V_DIM), lambda i, j: (i, j, 0)),
        out_shape=jax.ShapeDtypeStruct((b, n, MLA_HEADS * V_DIM), BF16),
        compiler_params=_cparams(("parallel", "arbitrary")),
        name="attn",
    )(q, k, v)


def _layer_norm(x, g, b):
    mu = jnp.mean(x, axis=-1, keepdims=True)
    xc = x - mu
    var = jnp.mean(xc * xc, axis=-1, keepdims=True)
    return xc * lax.rsqrt(var + EPS) * g + b


def _top16(s):
    nk, tt = s.shape
    iota = lax.broadcasted_iota(jnp.int32, (nk, tt), 0)
    iota16 = lax.broadcasted_iota(jnp.int32, (PEER_TOPK, tt), 0)
    rank = jnp.full((nk, tt), float(PEER_TOPK), F32)
    vals = jnp.zeros((PEER_TOPK, tt), F32)
    work = s
    for r in range(PEER_TOPK):
        m = jnp.max(work, axis=0, keepdims=True)
        idx = jnp.min(jnp.where(work == m, iota, nk), axis=0, keepdims=True)
        hit = iota == idx
        rank = jnp.where(hit, float(r), rank)
        work = jnp.where(hit, -jnp.inf, work)
        vals = jnp.where(iota16 == r, m, vals)
    return vals, rank


def _young_counts(v1, v2):
    tt = v1.shape[1]
    iota16 = lax.broadcasted_iota(jnp.int32, (PEER_TOPK, tt), 0)
    n = jnp.zeros((PEER_TOPK, tt), jnp.int32)
    f = v1 + v2[0:1, :]
    for r in range(PEER_TOPK):
        m = jnp.max(f, axis=0, keepdims=True)
        idx = jnp.min(jnp.where(f == m, iota16, PEER_TOPK), axis=0, keepdims=True)
        hit = iota16 == idx
        n = jnp.where(hit, n + 1, n)
        if r + 1 < PEER_TOPK:
            nxt = jnp.full((PEER_TOPK, tt), -jnp.inf, F32)
            for b in range(1, PEER_TOPK):
                nxt = jnp.where(n == b, v2[b:b + 1, :], nxt)
            f = jnp.where(hit, v1 + nxt, f)
    return n


def _post_kernel(x_ref, a_ref, b_ref, mod_ref, wo_ref, g1_ref, b1_ref, wpq_ref, sk_ref,
                 x1_ref, h2t_ref, r2_ref, e2_ref, n1_ref, e1z_ref, qt_scr):
    gate1 = mod_ref[0, 2:3, :]
    shift2 = mod_ref[0, 3:4, :]
    scale2 = mod_ref[0, 4:5, :]
    mix = _dot(a_ref[...], wo_ref[0:A_WIDTH, :]) + _dot(b_ref[...], wo_ref[A_WIDTH:2 * A_WIDTH, :])
    x1 = _layer_norm(ALPHA * x_ref[...] + gate1 * mix, g1_ref[...], b1_ref[...])
    x1_ref[...] = x1
    h2 = x1 * (1.0 + scale2) + shift2
    h2t = h2.T.astype(BF16)
    h2t_ref[...] = h2t
    qt_scr[...] = _dot(wpq_ref[...], h2t)

    def head_body(h, carry):
        r0 = pl.multiple_of(h * 2 * N_KEYS, 2 * N_KEYS)
        q1 = qt_scr[pl.ds(r0, N_KEYS), :].astype(BF16)
        q2 = qt_scr[pl.ds(r0 + N_KEYS, N_KEYS), :].astype(BF16)
        s1 = _dot(sk_ref[2 * h], q1)
        s2 = _dot(sk_ref[2 * h + 1], q2)
        v1, rk1 = _top16(s1)
        v2, rk2 = _top16(s2)
        n = _young_counts(v1, v2)
        e1t = jnp.exp(v1 - v1[0:1, :])
        e2t = jnp.exp(v2 - v2[0:1, :])
        za = jnp.zeros_like(e1t)
        for b in range(PEER_TOPK):
            za = za + jnp.where(n > b, e2t[b:b + 1, :], 0.0)
        inv_z = 1.0 / jnp.sum(e1t * za, axis=0, keepdims=True)
        nf = n.astype(F32)
        n1 = jnp.zeros_like(s1)
        for a in range(PEER_TOPK):
            n1 = jnp.where(rk1 == float(a), nf[a:a + 1, :], n1)
        r2_ref[h] = rk2.astype(BF16)
        e2_ref[h] = jnp.exp(s2 - v2[0:1, :]).astype(BF16)
        n1_ref[h] = n1
        e1z_ref[h] = jnp.exp(s1 - v1[0:1, :]) * inv_z
        return carry

    lax.fori_loop(0, PEER_HEADS, head_body, 0)


def _post_call(x, a_out, b_out, mod, mod_row_of_tile, w):
    t = x.shape[0]
    tt = POST_TT
    full = lambda shape: pl.BlockSpec(shape, lambda i: (0,) * len(shape))
    tok = lambda width: pl.BlockSpec((tt, width), lambda i: (i, 0))
    rout = pl.BlockSpec((PEER_HEADS, N_KEYS, tt), lambda i: (0, 0, i))
    return pl.pallas_call(
        _post_kernel,
        grid=(t // tt,),
        in_specs=[tok(D_MODEL), tok(A_WIDTH), tok(A_WIDTH),
                  pl.BlockSpec((1, 6, D_MODEL), lambda i: (mod_row_of_tile(i), 0, 0)),
                  full((D_MODEL, D_MODEL)), full((1, D_MODEL)), full((1, D_MODEL)),
                  full((PEER_HEADS * 2 * N_KEYS, D_MODEL)), full((PEER_HEADS * 2, N_KEYS, N_KEYS))],
        out_specs=[tok(D_MODEL), pl.BlockSpec((D_MODEL, tt), lambda i: (0, i)), rout, rout, rout, rout],
        out_shape=[jax.ShapeDtypeStruct((t, D_MODEL), F32),
                   jax.ShapeDtypeStruct((D_MODEL, t), BF16),
                   jax.ShapeDtypeStruct((PEER_HEADS, N_KEYS, t), BF16),
                   jax.ShapeDtypeStruct((PEER_HEADS, N_KEYS, t), BF16),
                   jax.ShapeDtypeStruct((PEER_HEADS, N_KEYS, t), F32),
                   jax.ShapeDtypeStruct((PEER_HEADS, N_KEYS, t), F32)],
        scratch_shapes=[pltpu.VMEM((PEER_HEADS * 2 * N_KEYS, tt), F32)],
        compiler_params=_cparams(("parallel",)),
        name="post_route",
    )(x, a_out, b_out, mod, w["w_o"], w["ln1_g"], w["ln1_b"], w["w_pqT"], w["sub_keys"])


def _peer_kernel(h2t_ref, r2_ref, e2_ref, n1_ref, e1z_ref, u_ref, vt_ref, x1_ref, mod_ref, g2_ref, b2_ref,
                 y_ref, a_scr, p_scr, acc_scr):
    j = pl.program_id(1)
    tt = h2t_ref.shape[1]
    n_i = u_ref.shape[0] // N_KEYS

    @pl.when(j == 0)
    def _():
        acc_scr[...] = jnp.zeros_like(acc_scr)

    a_scr[...] = _dot(u_ref[...], h2t_ref[...])

    for il in range(n_i):
        r0 = il * N_KEYS
        for c in range(tt // LANES):
            l0 = c * LANES
            wgt = jnp.zeros((N_KEYS, LANES), BF16)
            for h in range(PEER_HEADS):
                nrow = n1_ref[h, il:il + 1, l0:l0 + LANES].astype(BF16)
                erow = e1z_ref[h, il:il + 1, l0:l0 + LANES].astype(BF16)
                sel = jnp.where(r2_ref[h, :, l0:l0 + LANES] < nrow, e2_ref[h, :, l0:l0 + LANES],
                                jnp.zeros((), BF16))
                wgt = wgt + sel * erow
            act = _gelu(a_scr[r0:r0 + N_KEYS, l0:l0 + LANES])
            p_scr[r0:r0 + N_KEYS, l0:l0 + LANES] = act.astype(BF16) * wgt

    acc_scr[...] += _dot(vt_ref[...], p_scr[...])

    @pl.when(j == pl.num_programs(1) - 1)
    def _():
        gate2 = mod_ref[0, 5:6, :]
        peer = acc_scr[...].T
        y_ref[...] = _layer_norm(ALPHA * x1_ref[...] + gate2 * peer, g2_ref[...], b2_ref[...])


def _peer_call(h2t, r2, e2, n1, e1z, x1, mod, mod_row_of_tile, w):
    t = x1.shape[0]
    tt, et = PEER_TT, PEER_ET
    n_i = et // N_KEYS
    rfull = pl.BlockSpec((PEER_HEADS, N_KEYS, tt), lambda i, j: (0, 0, i))
    rrow = pl.BlockSpec((PEER_HEADS, n_i, tt), lambda i, j: (0, j, i))
    return pl.pallas_call(
        _peer_kernel,
        grid=(t // tt, N_EXPERTS // et),
        in_specs=[pl.BlockSpec((D_MODEL, tt), lambda i, j: (0, i)),
                  rfull, rfull, rrow, rrow,
                  pl.BlockSpec((et, D_MODEL), lambda i, j: (j, 0)),
                  pl.BlockSpec((D_MODEL, et), lambda i, j: (0, j)),
                  pl.BlockSpec((tt, D_MODEL), lambda i, j: (i, 0)),
                  pl.BlockSpec((1, 6, D_MODEL), lambda i, j: (mod_row_of_tile(i), 0, 0)),
                  pl.BlockSpec((1, D_MODEL), lambda i, j: (0, 0)),
                  pl.BlockSpec((1, D_MODEL), lambda i, j: (0, 0))],
        out_specs=pl.BlockSpec((tt, D_MODEL), lambda i, j: (i, 0)),
        out_shape=jax.ShapeDtypeStruct((t, D_MODEL), F32),
        scratch_shapes=[pltpu.VMEM((et, tt), F32), pltpu.VMEM((et, tt), BF16), pltpu.VMEM((D_MODEL, tt), F32)],
        compiler_params=_cparams(("parallel", "arbitrary")),
        name="peer_dense",
    )(h2t, r2, e2, n1, e1z, w["u_tab"], w["v_tabT"], x1, mod, w["ln2_g"], w["ln2_b"])


_SWAP64 = np.concatenate([np.arange(16, 32), np.arange(0, 16), np.arange(48, 64), np.arange(32, 48)])


def _rope_tables(n_pos):
    pos = jnp.arange(n_pos)
    row = (pos // GRID_W).astype(F32)
    col = (pos % GRID_W).astype(F32)
    f = QK_ROPE // 4
    freqs = 1.0 / (ROPE_THETA ** (jnp.arange(f, dtype=F32) / f))
    ar = row[:, None] * freqs[None, :]
    ac = col[:, None] * freqs[None, :]
    zeros = jnp.zeros((n_pos, QK_ROPE), F32)
    cos_t = jnp.concatenate([jnp.cos(ar), jnp.cos(ar), jnp.cos(ac), jnp.cos(ac), zeros], axis=1)
    sin_t = jnp.concatenate([-jnp.sin(ar), jnp.sin(ar), -jnp.sin(ac), jnp.sin(ac), zeros], axis=1)
    return cos_t, sin_t


def _identity_tables(n_pos):
    ones = jnp.ones((n_pos, QK_ROPE), F32)
    zeros = jnp.zeros((n_pos, QK_ROPE), F32)
    return jnp.concatenate([ones, zeros], axis=1), jnp.zeros((n_pos, LANES), F32)


def _prep_weights(w_in, w_s, b_s, g_q, w_qb, g_kv, w_kvb, w_o, ln1_g, ln1_b, w_pq, sub_keys, u_tab, v_tab,
                  ln2_g, ln2_b):
    o_kr = 2 * A_WIDTH + Q_RANK + KV_RANK
    w_in_ext = jnp.concatenate([w_in, w_in[:, o_kr + _SWAP64]], axis=1).astype(BF16)
    wq = w_qb.reshape(Q_RANK, MLA_HEADS, QK_NOPE + QK_ROPE)
    w_qb_ext = jnp.concatenate([wq, wq[:, :, QK_NOPE + _SWAP64]], axis=2).reshape(Q_RANK, MLA_HEADS * HEAD_W)
    return {
        "w_in": w_in_ext,
        "w_s": w_s.astype(BF16),
        "bsb": jnp.broadcast_to(b_s[:, :, None], (A_GROUPS, CHUNK, LANES)),
        "g_q": g_q[None, :],
        "w_qb": w_qb_ext.astype(BF16),
        "g_kv": g_kv[None, :],
        "w_kvb": w_kvb.astype(BF16),
        "w_o": w_o.astype(BF16),
        "ln1_g": ln1_g[None, :],
        "ln1_b": ln1_b[None, :],
        "w_pqT": w_pq.T.astype(BF16),
        "sub_keys": sub_keys.reshape(PEER_HEADS * 2, N_KEYS, N_KEYS).astype(BF16),
        "u_tab": u_tab.astype(BF16),
        "v_tabT": v_tab.T.astype(BF16),
        "ln2_g": ln2_g[None, :],
        "ln2_b": ln2_b[None, :],
    }


def _trunk(x, mod, mod_row_of_seq, w, cos_t, sin_t, cache):
    b, n, _ = x.shape
    xf = x.reshape(b * n, D_MODEL)
    row_of = lambda tile_tokens: (lambda i: mod_row_of_seq(i // (n // tile_tokens)))
    a_out, q, k, v, ckv, kr = _pre_call(xf, mod, row_of(PRE_TT), n // PRE_TT, w, cos_t, sin_t)
    q = q.reshape(b, n, -1)
    k = k.reshape(b, n, -1)
    v = v.reshape(b, n, -1)
    if cache is not None:
        k = jnp.concatenate([cache[0], k], axis=1)
        v = jnp.concatenate([cache[1], v], axis=1)
    b_out = _attn_call(q, k, v).reshape(b * n, -1)
    x1, h2t, r2, e2, n1, e1z = _post_call(xf, a_out, b_out, mod, row_of(POST_TT), w)
    y = _peer_call(h2t, r2, e2, n1, e1z, x1, mod, row_of(PEER_TT), w)
    return y.reshape(b, n, D_MODEL), ckv, kr


def kernel(x_prompt, x_sample, cache_ckv, cache_krope, c, c_ctx, w_mod, b_mod, w_in, w_s, b_s, g_q, w_qb, g_kv,
           w_kvb, w_o, ln1_g, ln1_b, w_pq, sub_keys, u_tab, v_tab, ln2_g, ln2_b):
    batch, seq, _ = x_prompt.shape
    dec_batch, dec_seq, _ = x_sample.shape
    past = cache_ckv.shape[2]
    y_p, y_s = x_prompt, x_sample
    ckv_list, kr_list = [], []
    for l in range(DEPTH):
        w = _prep_weights(w_in[l], w_s[l], b_s[l], g_q[l], w_qb[l], g_kv[l], w_kvb[l], w_o[l], ln1_g[l], ln1_b[l],
                          w_pq[l], sub_keys[l], u_tab[l], v_tab[l], ln2_g[l], ln2_b[l])
        c_rows = jnp.concatenate([c_ctx[None, :], c, jnp.zeros((8 - 1 - dec_batch, D_MODEL), F32)], axis=0)
        mod = _mod_call(c_rows, w_mod[l], b_mod[l][None, :]).reshape(8, 6, D_MODEL)

        cos_c, sin_c = _identity_tables(seq)
        y_p, ckv_l, kr_l = _trunk(y_p, mod, lambda s: 0, w, cos_c, sin_c, None)
        ckv_list.append(ckv_l.reshape(batch, seq, KV_RANK))
        kr_list.append(kr_l.reshape(batch, seq, QK_ROPE))

        krp = jnp.pad(cache_krope[:, l].reshape(dec_batch * past, QK_ROPE), ((0, 0), (0, LANES - QK_ROPE)))
        k_c, v_c = _cache_call(cache_ckv[:, l].reshape(dec_batch * past, KV_RANK), krp, w["w_kvb"])
        cache = (k_c.reshape(dec_batch, past, -1), v_c.reshape(dec_batch, past, -1))
        cos_l, sin_l = _rope_tables(dec_seq)
        y_s, _, _ = _trunk(y_s, mod, lambda s: 1 + s, w, cos_l, sin_l, cache)
    new_ckv = jnp.stack(ckv_list, axis=1)
    new_krope = jnp.stack(kr_list, axis=1)
    return (y_p, y_s, new_ckv, new_krope)
```

```python
import functools
import math

import jax
import jax.numpy as jnp
import numpy as np
from jax import lax
from jax.experimental import pallas as pl
from jax.experimental.pallas import tpu as pltpu

F32 = jnp.float32
BF16 = jnp.bfloat16

D_MODEL = 1024
GRID_W = 64
CHUNK = 128
A_GROUPS = 4
A_WIDTH = 512
MLA_HEADS = 4
QK_NOPE = 128
QK_ROPE = 64
V_DIM = 128
Q_RANK = 384
KV_RANK = 256
ROPE_THETA = 10000.0
N_KEYS = 128
PEER_HEADS = 8
PEER_TOPK = 16
N_EXPERTS = N_KEYS * N_KEYS
DEPTH = 1
ALPHA = (2.0 * DEPTH) ** 0.25
EPS = 1e-6

LANES = 128
HEAD_W = 2 * LANES
IN_EXT = 2 * A_WIDTH + Q_RANK + KV_RANK + 2 * QK_ROPE

PRE_TT = 256
ATT_TQ = 256
POST_TT = 256
ROUTE_GROUP = 2
PEER_TT = 512
PEER_ET = 1024
PEER_QT = 256
VMEM_LIMIT = 56 * 1024 * 1024


def _cparams(sem):
    return pltpu.CompilerParams(dimension_semantics=sem, vmem_limit_bytes=VMEM_LIMIT)


def _gelu(x):
    return x * (lax.erf(x * (1.0 / math.sqrt(2.0))) + 1.0) * 0.5


def _dot(a, b):
    return jnp.dot(a, b, preferred_element_type=F32)


def _mod_kernel(c_ref, w_ref, b_ref, o_ref):
    c = c_ref[...]
    sc = c * (1.0 / (1.0 + jnp.exp(-c)))
    o_ref[...] = _dot(sc.astype(BF16), w_ref[...].astype(BF16)) + b_ref[...]


def _mod_call(c_rows, w_mod, b_mod):
    n = w_mod.shape[1]
    bn = 1536
    return pl.pallas_call(
        _mod_kernel,
        grid=(n // bn,),
        in_specs=[pl.BlockSpec((8, D_MODEL), lambda j: (0, 0)),
                  pl.BlockSpec((D_MODEL, bn), lambda j: (0, j)),
                  pl.BlockSpec((1, bn), lambda j: (0, j))],
        out_specs=pl.BlockSpec((8, bn), lambda j: (0, j)),
        out_shape=jax.ShapeDtypeStruct((8, n), F32),
        compiler_params=_cparams(("arbitrary",)),
        name="mod",
    )(c_rows, w_mod, b_mod)


def _rms(x, g):
    return x * lax.rsqrt(jnp.mean(x * x, axis=-1, keepdims=True) + EPS) * g


def _pre_kernel(x_ref, mod_ref, win_ref, ws_ref, bsb_ref, gq_ref, wqb_ref, gkv_ref, wkvb_ref,
                cos_ref, sin_ref, a_ref, q_ref, k_ref, v_ref, ckv_ref, kr_ref):
    x = x_ref[...]
    shift1 = mod_ref[0, 0:1, :]
    scale1 = mod_ref[0, 1:2, :]
    h = x * (1.0 + scale1) + shift1
    z = _dot(h.astype(BF16), win_ref[...])
    tt = x.shape[0]

    for g in range(A_GROUPS):
        lo = g * LANES
        ug = _gelu(z[:, lo:lo + LANES])
        vg = _gelu(z[:, A_WIDTH + lo:A_WIDTH + lo + LANES])
        mu = jnp.mean(vg, axis=-1, keepdims=True)
        vc = vg - mu
        var = jnp.mean(vc * vc, axis=-1, keepdims=True)
        vs = (vc * lax.rsqrt(var + EPS)).astype(BF16)
        for c in range(tt // CHUNK):
            r0 = c * CHUNK
            mixed = _dot(ws_ref[g], vs[r0:r0 + CHUNK, :]) + bsb_ref[g]
            a_ref[r0:r0 + CHUNK, lo:lo + LANES] = (ug[r0:r0 + CHUNK, :] * mixed).astype(BF16)

    cos = cos_ref[...]
    sin = sin_ref[...]
    o_q = 2 * A_WIDTH
    qn = _rms(z[:, o_q:o_q + Q_RANK], gq_ref[...])
    zq = _dot(qn.astype(BF16), wqb_ref[...])
    for hd in range(MLA_HEADS):
        c0 = hd * HEAD_W
        q_ref[:, c0:c0 + LANES] = zq[:, c0:c0 + LANES].astype(BF16)
        blk = zq[:, c0 + LANES:c0 + HEAD_W]
        q_ref[:, c0 + LANES:c0 + HEAD_W] = (blk * cos + pltpu.roll(blk, 64, 1) * sin).astype(BF16)

    o_kv = o_q + Q_RANK
    ckv = _rms(z[:, o_kv:o_kv + KV_RANK], gkv_ref[...])
    ckv_ref[...] = ckv
    o_kr = o_kv + KV_RANK
    kr_ref[...] = z[:, o_kr:o_kr + QK_ROPE]
    kv = _dot(ckv.astype(BF16), wkvb_ref[...])
    kblk = z[:, o_kr:o_kr + LANES]
    yk = (kblk * cos + pltpu.roll(kblk, 64, 1) * sin).astype(BF16)
    for hd in range(MLA_HEADS):
        c0 = hd * HEAD_W
        k_ref[:, c0:c0 + LANES] = kv[:, c0:c0 + LANES].astype(BF16)
        k_ref[:, c0 + LANES:c0 + HEAD_W] = yk
        v_ref[:, hd * V_DIM:(hd + 1) * V_DIM] = kv[:, c0 + LANES:c0 + HEAD_W].astype(BF16)


def _pre_call(x, mod, mod_row_of_tile, pos_tiles, w, cos_t, sin_t):
    t = x.shape[0]
    tt = PRE_TT
    full = lambda shape: pl.BlockSpec(shape, lambda i: (0,) * len(shape))
    tok = lambda width: pl.BlockSpec((tt, width), lambda i: (i, 0))
    return pl.pallas_call(
        _pre_kernel,
        grid=(t // tt,),
        in_specs=[tok(D_MODEL),
                  pl.BlockSpec((1, 6, D_MODEL), lambda i: (mod_row_of_tile(i), 0, 0)),
                  full((D_MODEL, IN_EXT)), full((A_GROUPS, CHUNK, CHUNK)), full((A_GROUPS, CHUNK, LANES)),
                  full((1, Q_RANK)), full((Q_RANK, MLA_HEADS * HEAD_W)),
                  full((1, KV_RANK)), full((KV_RANK, MLA_HEADS * HEAD_W)),
                  pl.BlockSpec((tt, LANES), lambda i: (i % pos_tiles, 0)),
                  pl.BlockSpec((tt, LANES), lambda i: (i % pos_tiles, 0))],
        out_specs=[tok(A_WIDTH), tok(MLA_HEADS * HEAD_W), tok(MLA_HEADS * HEAD_W), tok(MLA_HEADS * V_DIM),
                   tok(KV_RANK), tok(QK_ROPE)],
        out_shape=[jax.ShapeDtypeStruct((t, A_WIDTH), BF16),
                   jax.ShapeDtypeStruct((t, MLA_HEADS * HEAD_W), BF16),
                   jax.ShapeDtypeStruct((t, MLA_HEADS * HEAD_W), BF16),
                   jax.ShapeDtypeStruct((t, MLA_HEADS * V_DIM), BF16),
                   jax.ShapeDtypeStruct((t, KV_RANK), F32),
                   jax.ShapeDtypeStruct((t, QK_ROPE), F32)],
        compiler_params=_cparams(("parallel",)),
        name="pre",
    )(x, mod, w["w_in"], w["w_s"], w["bsb"], w["g_q"], w["w_qb"], w["g_kv"], w["w_kvb"], cos_t, sin_t)


def _cache_kernel(ckv_ref, krp_ref, wkvb_ref, k_ref, v_ref):
    kv = _dot(ckv_ref[...].astype(BF16), wkvb_ref[...])
    yk = krp_ref[...].astype(BF16)
    for hd in range(MLA_HEADS):
        c0 = hd * HEAD_W
        k_ref[:, c0:c0 + LANES] = kv[:, c0:c0 + LANES].astype(BF16)
        k_ref[:, c0 + LANES:c0 + HEAD_W] = yk
        v_ref[:, hd * V_DIM:(hd + 1) * V_DIM] = kv[:, c0 + LANES:c0 + HEAD_W].astype(BF16)


def _cache_call(ckv, krp, w_kvb):
    t = ckv.shape[0]
    tt = 256
    return pl.pallas_call(
        _cache_kernel,
        grid=(t // tt,),
        in_specs=[pl.BlockSpec((tt, KV_RANK), lambda i: (i, 0)),
                  pl.BlockSpec((tt, LANES), lambda i: (i, 0)),
                  pl.BlockSpec((KV_RANK, MLA_HEADS * HEAD_W), lambda i: (0, 0))],
        out_specs=[pl.BlockSpec((tt, MLA_HEADS * HEAD_W), lambda i: (i, 0)),
                   pl.BlockSpec((tt, MLA_HEADS * V_DIM), lambda i: (i, 0))],
        out_shape=[jax.ShapeDtypeStruct((t, MLA_HEADS * HEAD_W), BF16),
                   jax.ShapeDtypeStruct((t, MLA_HEADS * V_DIM), BF16)],
        compiler_params=_cparams(("parallel",)),
        name="cache_kv",
    )(ckv, krp, w_kvb)


def _attn_kernel(q_ref, k_ref, v_ref, o_ref):
    scale = 1.0 / math.sqrt(QK_NOPE + QK_ROPE)
    for hd in range(MLA_HEADS):
        c0 = hd * HEAD_W
        q = q_ref[0, :, c0:c0 + HEAD_W]
        k = k_ref[0, :, c0:c0 + HEAD_W]
        s = lax.dot_general(q, k, (((1,), (1,)), ((), ())), preferred_element_type=F32) * scale
        m = jnp.max(s, axis=-1, keepdims=True)
        p = jnp.exp(s - m)
        l = jnp.sum(p, axis=-1, keepdims=True)
        o = _dot(p.astype(BF16), v_ref[0, :, hd * V_DIM:(hd + 1) * V_DIM])
        o_ref[0, :, hd * V_DIM:(hd + 1) * V_DIM] = (o / l).astype(BF16)


def _attn_call(q, k, v):
    b, n, _ = q.shape
    m = k.shape[1]
    tq = ATT_TQ
    return pl.pallas_call(
        _attn_kernel,
        grid=(b, n // tq),
        in_specs=[pl.BlockSpec((1, tq, MLA_HEADS * HEAD_W), lambda i, j: (i, j, 0)),
                  pl.BlockSpec((1, m, MLA_HEADS * HEAD_W), lambda i, j: (i, 0, 0)),
                  pl.BlockSpec((1, m, MLA_HEADS * V_DIM), lambda i, j: (i, 0, 0))],
        out_specs=pl.BlockSpec((1, tq, MLA_HEADS * V_DIM), lambda i, j: (i, j, 0)),
        out_shape=jax.ShapeDtypeStruct((b, n, MLA_HEADS * V_DIM), BF16),
        compiler_params=_cparams(("parallel", "arbitrary")),
        name="attn",
    )(q, k, v)


def _layer_norm(x, g, b):
    mu = jnp.mean(x, axis=-1, keepdims=True)
    xc = x - mu
    var = jnp.mean(xc * xc, axis=-1, keepdims=True)
    return xc * lax.rsqrt(var + EPS) * g + b


def _first_max(x, exact):
    m = jnp.max(x, axis=0, keepdims=True)
    hit = x == m
    if exact:
        iota = lax.broadcasted_iota(jnp.int32, x.shape, 0)
        hit = iota == jnp.min(jnp.where(hit, iota, x.shape[0]), axis=0, keepdims=True)
    return m, hit


def _top16(s, exact, want_rank=True):
    nk, tt = s.shape
    iota16 = lax.broadcasted_iota(jnp.int32, (PEER_TOPK, tt), 0)
    rank = jnp.full((nk, tt), float(PEER_TOPK), F32) if want_rank else None
    vals = jnp.zeros((PEER_TOPK, tt), F32)
    work = s
    for r in range(PEER_TOPK):
        m, hit = _first_max(work, exact)
        if want_rank:
            rank = jnp.where(hit, float(r), rank)
        work = jnp.where(hit, -jnp.inf, work)
        vals = jnp.where(iota16 == r, m, vals)
    return vals, rank


def _young_counts(v1, v2, exact):
    n = jnp.zeros(v1.shape, F32)
    f = v1 + v2[0:1, :]
    for r in range(PEER_TOPK):
        _, hit = _first_max(f, exact)
        n = jnp.where(hit, n + 1.0, n)
        if r + 1 < PEER_TOPK:
            taken = jnp.sum(jnp.where(hit, n, 0.0), axis=0, keepdims=True)
            nxt = jnp.full(taken.shape, -jnp.inf, F32)
            for b in range(1, PEER_TOPK):
                nxt = jnp.where(taken == float(b), v2[b:b + 1, :], nxt)
            f = jnp.where(hit, v1 + nxt, f)
    return n


def _route_head(s1, s2, exact):
    v1, rk1 = _top16(s1, exact, want_rank=exact)
    v2, rk2 = _top16(s2, exact)
    n = _young_counts(v1, v2, exact)
    e1t = jnp.exp(v1 - v1[0:1, :])
    e2t = jnp.exp(v2 - v2[0:1, :])
    za = jnp.zeros_like(e1t)
    for b in range(PEER_TOPK):
        za = za + jnp.where(n > float(b), e2t[b:b + 1, :], 0.0)
    inv_z = 1.0 / jnp.sum(e1t * za, axis=0, keepdims=True)
    n1 = jnp.zeros_like(s1)
    for a in range(PEER_TOPK):
        is_a = (rk1 == float(a)) if exact else (s1 == v1[a:a + 1, :])
        n1 = jnp.where(is_a, n[a:a + 1, :], n1)
    e2 = jnp.exp(s2 - v2[0:1, :])
    e1z = jnp.exp(s1 - v1[0:1, :]) * inv_z
    count = lambda mask: jnp.sum(jnp.where(mask, 1.0, 0.0), axis=0, keepdims=True)
    k = float(PEER_TOPK)
    picked1 = count(s1 >= v1[PEER_TOPK - 1:PEER_TOPK, :])
    tie = (picked1 != k) | (count(rk2 < k) != k) | (jnp.sum(n, axis=0, keepdims=True) != k)
    return rk2, e2, n1, e1z, tie


def _post_kernel(x_ref, a_ref, b_ref, mod_ref, wo_ref, g1_ref, b1_ref, wpq_ref, sk_ref,
                 x1_ref, h2t_ref, r2_ref, e2_ref, n1_ref, e1z_ref, qt_scr, s_scr):
    gate1 = mod_ref[0, 2:3, :]
    shift2 = mod_ref[0, 3:4, :]
    scale2 = mod_ref[0, 4:5, :]
    mix = _dot(a_ref[...], wo_ref[0:A_WIDTH, :]) + _dot(b_ref[...], wo_ref[A_WIDTH:2 * A_WIDTH, :])
    x1 = _layer_norm(ALPHA * x_ref[...] + gate1 * mix, g1_ref[...], b1_ref[...])
    x1_ref[...] = x1
    h2 = x1 * (1.0 + scale2) + shift2
    h2t = h2.T.astype(BF16)
    h2t_ref[...] = h2t
    qt_scr[...] = _dot(wpq_ref[...], h2t)

    def group_body(g, carry):
        heads = [g * ROUTE_GROUP + i for i in range(ROUTE_GROUP)]
        for i, h in enumerate(heads):
            r0 = pl.multiple_of(h * 2 * N_KEYS, 2 * N_KEYS)
            q1 = qt_scr[pl.ds(r0, N_KEYS), :].astype(BF16)
            q2 = qt_scr[pl.ds(r0 + N_KEYS, N_KEYS), :].astype(BF16)
            s_scr[2 * i] = _dot(sk_ref[2 * h], q1)
            s_scr[2 * i + 1] = _dot(sk_ref[2 * h + 1], q2)

        def emit(i, h, exact):
            rk2, e2, n1, e1z, tie = _route_head(s_scr[2 * i], s_scr[2 * i + 1], exact)
            r2_ref[h] = rk2.astype(BF16)
            e2_ref[h] = e2.astype(BF16)
            n1_ref[h] = n1
            e1z_ref[h] = e1z
            return jnp.max(jnp.where(tie, 1.0, 0.0))

        ties = [emit(i, h, False) for i, h in enumerate(heads)]
        for i, h in enumerate(heads):
            @pl.when(ties[i] > 0.0)
            def _(i=i, h=h):
                emit(i, h, True)

        return carry

    lax.fori_loop(0, PEER_HEADS // ROUTE_GROUP, group_body, 0)


def _post_call(x, a_out, b_out, mod, mod_row_of_tile, w):
    t = x.shape[0]
    tt = POST_TT
    full = lambda shape: pl.BlockSpec(shape, lambda i: (0,) * len(shape))
    tok = lambda width: pl.BlockSpec((tt, width), lambda i: (i, 0))
    rout = pl.BlockSpec((PEER_HEADS, N_KEYS, tt), lambda i: (0, 0, i))
    return pl.pallas_call(
        _post_kernel,
        grid=(t // tt,),
        in_specs=[tok(D_MODEL), tok(A_WIDTH), tok(A_WIDTH),
                  pl.BlockSpec((1, 6, D_MODEL), lambda i: (mod_row_of_tile(i), 0, 0)),
                  full((D_MODEL, D_MODEL)), full((1, D_MODEL)), full((1, D_MODEL)),
                  full((PEER_HEADS * 2 * N_KEYS, D_MODEL)), full((PEER_HEADS * 2, N_KEYS, N_KEYS))],
        out_specs=[tok(D_MODEL), pl.BlockSpec((D_MODEL, tt), lambda i: (0, i)), rout, rout, rout, rout],
        out_shape=[jax.ShapeDtypeStruct((t, D_MODEL), F32),
                   jax.ShapeDtypeStruct((D_MODEL, t), BF16),
                   jax.ShapeDtypeStruct((PEER_HEADS, N_KEYS, t), BF16),
                   jax.ShapeDtypeStruct((PEER_HEADS, N_KEYS, t), BF16),
                   jax.ShapeDtypeStruct((PEER_HEADS, N_KEYS, t), F32),
                   jax.ShapeDtypeStruct((PEER_HEADS, N_KEYS, t), F32)],
        scratch_shapes=[pltpu.VMEM((PEER_HEADS * 2 * N_KEYS, tt), F32),
                        pltpu.VMEM((2 * ROUTE_GROUP, N_KEYS, tt), F32)],
        compiler_params=_cparams(("parallel",)),
        name="post_route",
    )(x, a_out, b_out, mod, w["w_o"], w["ln1_g"], w["ln1_b"], w["w_pqT"], w["sub_keys"])


BF16_ROWS = 16


def _peer_phases(h2t_ref, r2_ref, e2_ref, n1_ref, e1z_ref, u_ref, vt_ref, a_w, a_r, p_w, p_r, acc_scr, valid):
    n_c = h2t_ref.shape[1] // LANES
    n_slab = N_KEYS // BF16_ROWS
    bcast16 = lambda row: jnp.broadcast_to(row, (BF16_ROWS, LANES)).astype(BF16)
    p_all = jnp.concatenate([p_r[c] for c in range(n_c)], axis=1)
    for q in range(u_ref.shape[0] // PEER_QT):
        q0 = q * PEER_QT
        pre = _dot(u_ref[q0:q0 + PEER_QT, :], h2t_ref[...])
        for c in range(n_c):
            a_w[c, q0:q0 + PEER_QT, :] = pre[:, c * LANES:(c + 1) * LANES]
        d0 = q * (D_MODEL // (u_ref.shape[0] // PEER_QT))
        d1 = (q + 1) * (D_MODEL // (u_ref.shape[0] // PEER_QT))
        acc_scr[d0:d1, :] += _dot(vt_ref[d0:d1, :], p_all)

        ils = range(q0 // N_KEYS, (q0 + PEER_QT) // N_KEYS)
        for c in range(n_c):
            l0 = c * LANES
            wgt = {il: [None] * n_slab for il in ils}
            for h in range(PEER_HEADS):
                rows = {il: (bcast16(n1_ref[h, il:il + 1, l0:l0 + LANES] * valid),
                             bcast16(e1z_ref[h, il:il + 1, l0:l0 + LANES])) for il in ils}
                for k in range(n_slab):
                    k0 = k * BF16_ROWS
                    r2s = r2_ref[h, c, k0:k0 + BF16_ROWS, :]
                    e2s = e2_ref[h, c, k0:k0 + BF16_ROWS, :]
                    for il in ils:
                        n16, e16 = rows[il]
                        term = jnp.where(r2s < n16, e2s, jnp.zeros((), BF16)) * e16
                        wgt[il][k] = term if wgt[il][k] is None else wgt[il][k] + term
            for il in ils:
                for k in range(n_slab):
                    k0 = il * N_KEYS + k * BF16_ROWS
                    act = _gelu(a_r[c, k0:k0 + BF16_ROWS, :])
                    p_w[c, k0:k0 + BF16_ROWS, :] = act.astype(BF16) * wgt[il][k]


def _peer_kernel(h2t_ref, r2_ref, e2_ref, n1_ref, e1z_ref, u_ref, vt_ref, x1_ref, mod_ref, g2_ref, b2_ref,
                 y_ref, a0, a1, p0, p1, acc_scr, r2_scr, e2_scr, *, n_e, n_blocks):
    s = pl.program_id(0)

    @pl.when(s == 0)
    def _():
        a1[...] = jnp.zeros_like(a1)
        p0[...] = jnp.zeros_like(p0)
        p1[...] = jnp.zeros_like(p1)

    @pl.when(jnp.clip(s - 2, 0, n_blocks - 1) % n_e == 0)
    def _():
        acc_scr[...] = jnp.zeros_like(acc_scr)

    @pl.when(jnp.clip(s - 1, 0, n_blocks - 1) % n_e == 0)
    def _():
        for c in range(r2_scr.shape[1]):
            r2_scr[:, c] = r2_ref[:, :, c * LANES:(c + 1) * LANES]
            e2_scr[:, c] = e2_ref[:, :, c * LANES:(c + 1) * LANES]

    valid = jnp.where((s >= 1) & (s <= n_blocks), 1.0, 0.0).astype(F32)
    args = (h2t_ref, r2_scr, e2_scr, n1_ref, e1z_ref, u_ref, vt_ref)

    @pl.when(s % 2 == 0)
    def _():
        _peer_phases(*args, a0, a1, p1, p0, acc_scr, valid)

    @pl.when(s % 2 == 1)
    def _():
        _peer_phases(*args, a1, a0, p0, p1, acc_scr, valid)

    @pl.when((s >= 2) & ((s - 2) % n_e == n_e - 1))
    def _():
        gate2 = mod_ref[0, 5:6, :]
        peer = acc_scr[...].T
        y_ref[...] = _layer_norm(ALPHA * x1_ref[...] + gate2 * peer, g2_ref[...], b2_ref[...])


def _peer_call(h2t, r2, e2, n1, e1z, x1, mod, mod_row_of_tile, w):
    t = x1.shape[0]
    tt, et = PEER_TT, PEER_ET
    n_i = et // N_KEYS
    n_e = N_EXPERTS // et
    n_blocks = (t // tt) * n_e
    n_c = tt // LANES
    blk = lambda s, lag: jnp.clip(s - lag, 0, n_blocks - 1)
    tile = lambda s, lag: blk(s, lag) // n_e
    exp = lambda s, lag: blk(s, lag) % n_e
    rfull = pl.BlockSpec((PEER_HEADS, N_KEYS, tt), lambda s: (0, 0, tile(s, 1)))
    rrow = pl.BlockSpec((PEER_HEADS, n_i, tt), lambda s: (0, exp(s, 1), tile(s, 1)))
    return pl.pallas_call(
        functools.partial(_peer_kernel, n_e=n_e, n_blocks=n_blocks),
        grid=(n_blocks + 2,),
        in_specs=[pl.BlockSpec((D_MODEL, tt), lambda s: (0, tile(s, 0))),
                  rfull, rfull, rrow, rrow,
                  pl.BlockSpec((et, D_MODEL), lambda s: (exp(s, 0), 0)),
                  pl.BlockSpec((D_MODEL, et), lambda s: (0, exp(s, 2))),
                  pl.BlockSpec((tt, D_MODEL), lambda s: (tile(s, 2), 0)),
                  pl.BlockSpec((1, 6, D_MODEL), lambda s: (mod_row_of_tile(tile(s, 2)), 0, 0)),
                  pl.BlockSpec((1, D_MODEL), lambda s: (0, 0)),
                  pl.BlockSpec((1, D_MODEL), lambda s: (0, 0))],
        out_specs=pl.BlockSpec((tt, D_MODEL), lambda s: (tile(s, 2), 0)),
        out_shape=jax.ShapeDtypeStruct((t, D_MODEL), F32),
        scratch_shapes=[pltpu.VMEM((n_c, et, LANES), F32), pltpu.VMEM((n_c, et, LANES), F32),
                        pltpu.VMEM((n_c, et, LANES), BF16), pltpu.VMEM((n_c, et, LANES), BF16),
                        pltpu.VMEM((D_MODEL, tt), F32),
                        pltpu.VMEM((PEER_HEADS, n_c, N_KEYS, LANES), BF16),
                        pltpu.VMEM((PEER_HEADS, n_c, N_KEYS, LANES), BF16)],
        compiler_params=_cparams(("arbitrary",)),
        name="peer_dense",
    )(h2t, r2, e2, n1, e1z, w["u_tab"], w["v_tabT"], x1, mod, w["ln2_g"], w["ln2_b"])


_SWAP64 = np.concatenate([np.arange(16, 32), np.arange(0, 16), np.arange(48, 64), np.arange(32, 48)])


def _rope_tables(n_pos):
    pos = jnp.arange(n_pos)
    row = (pos // GRID_W).astype(F32)
    col = (pos % GRID_W).astype(F32)
    f = QK_ROPE // 4
    freqs = 1.0 / (ROPE_THETA ** (jnp.arange(f, dtype=F32) / f))
    ar = row[:, None] * freqs[None, :]
    ac = col[:, None] * freqs[None, :]
    zeros = jnp.zeros((n_pos, QK_ROPE), F32)
    cos_t = jnp.concatenate([jnp.cos(ar), jnp.cos(ar), jnp.cos(ac), jnp.cos(ac), zeros], axis=1)
    sin_t = jnp.concatenate([-jnp.sin(ar), jnp.sin(ar), -jnp.sin(ac), jnp.sin(ac), zeros], axis=1)
    return cos_t, sin_t


def _identity_tables(n_pos):
    ones = jnp.ones((n_pos, QK_ROPE), F32)
    zeros = jnp.zeros((n_pos, QK_ROPE), F32)
    return jnp.concatenate([ones, zeros], axis=1), jnp.zeros((n_pos, LANES), F32)


def _prep_weights(w_in, w_s, b_s, g_q, w_qb, g_kv, w_kvb, w_o, ln1_g, ln1_b, w_pq, sub_keys, u_tab, v_tab,
                  ln2_g, ln2_b):
    o_kr = 2 * A_WIDTH + Q_RANK + KV_RANK
    w_in_ext = jnp.concatenate([w_in, w_in[:, o_kr + _SWAP64]], axis=1).astype(BF16)
    wq = w_qb.reshape(Q_RANK, MLA_HEADS, QK_NOPE + QK_ROPE)
    w_qb_ext = jnp.concatenate([wq, wq[:, :, QK_NOPE + _SWAP64]], axis=2).reshape(Q_RANK, MLA_HEADS * HEAD_W)
    return {
        "w_in": w_in_ext,
        "w_s": w_s.astype(BF16),
        "bsb": jnp.broadcast_to(b_s[:, :, None], (A_GROUPS, CHUNK, LANES)),
        "g_q": g_q[None, :],
        "w_qb": w_qb_ext.astype(BF16),
        "g_kv": g_kv[None, :],
        "w_kvb": w_kvb.astype(BF16),
        "w_o": w_o.astype(BF16),
        "ln1_g": ln1_g[None, :],
        "ln1_b": ln1_b[None, :],
        "w_pqT": w_pq.T.astype(BF16),
        "sub_keys": sub_keys.reshape(PEER_HEADS * 2, N_KEYS, N_KEYS).astype(BF16),
        "u_tab": u_tab.astype(BF16),
        "v_tabT": v_tab.T.astype(BF16),
        "ln2_g": ln2_g[None, :],
        "ln2_b": ln2_b[None, :],
    }


def _trunk(x, mod, mod_row_of_seq, w, cos_t, sin_t, cache):
    b, n, _ = x.shape
    xf = x.reshape(b * n, D_MODEL)
    row_of = lambda tile_tokens: (lambda i: mod_row_of_seq(i // (n // tile_tokens)))
    a_out, q, k, v, ckv, kr = _pre_call(xf, mod, row_of(PRE_TT), n // PRE_TT, w, cos_t, sin_t)
    q = q.reshape(b, n, -1)
    k = k.reshape(b, n, -1)
    v = v.reshape(b, n, -1)
    if cache is not None:
        k = jnp.concatenate([cache[0], k], axis=1)
        v = jnp.concatenate([cache[1], v], axis=1)
    b_out = _attn_call(q, k, v).reshape(b * n, -1)
    x1, h2t, r2, e2, n1, e1z = _post_call(xf, a_out, b_out, mod, row_of(POST_TT), w)
    y = _peer_call(h2t, r2, e2, n1, e1z, x1, mod, row_of(PEER_TT), w)
    return y.reshape(b, n, D_MODEL), ckv, kr


def kernel(x_prompt, x_sample, cache_ckv, cache_krope, c, c_ctx, w_mod, b_mod, w_in, w_s, b_s, g_q, w_qb, g_kv,
           w_kvb, w_o, ln1_g, ln1_b, w_pq, sub_keys, u_tab, v_tab, ln2_g, ln2_b):
    batch, seq, _ = x_prompt.shape
    dec_batch, dec_seq, _ = x_sample.shape
    past = cache_ckv.shape[2]
    y_p, y_s = x_prompt, x_sample
    ckv_list, kr_list = [], []
    for l in range(DEPTH):
        w = _prep_weights(w_in[l], w_s[l], b_s[l], g_q[l], w_qb[l], g_kv[l], w_kvb[l], w_o[l], ln1_g[l], ln1_b[l],
                          w_pq[l], sub_keys[l], u_tab[l], v_tab[l], ln2_g[l], ln2_b[l])
        c_rows = jnp.concatenate([c_ctx[None, :], c, jnp.zeros((8 - 1 - dec_batch, D_MODEL), F32)], axis=0)
        mod = _mod_call(c_rows, w_mod[l], b_mod[l][None, :]).reshape(8, 6, D_MODEL)

        cos_c, sin_c = _identity_tables(seq)
        y_p, ckv_l, kr_l = _trunk(y_p, mod, lambda s: 0, w, cos_c, sin_c, None)
        ckv_list.append(ckv_l.reshape(batch, seq, KV_RANK))
        kr_list.append(kr_l.reshape(batch, seq, QK_ROPE))

        krp = jnp.pad(cache_krope[:, l].reshape(dec_batch * past, QK_ROPE), ((0, 0), (0, LANES - QK_ROPE)))
        k_c, v_c = _cache_call(cache_ckv[:, l].reshape(dec_batch * past, KV_RANK), krp, w["w_kvb"])
        cache = (k_c.reshape(dec_batch, past, -1), v_c.reshape(dec_batch, past, -1))
        cos_l, sin_l = _rope_tables(dec_seq)
        y_s, _, _ = _trunk(y_s, mod, lambda s: 1 + s, w, cos_l, sin_l, cache)
    new_ckv = jnp.stack(ckv_list, axis=1)
    new_krope = jnp.stack(kr_list, axis=1)
    return (y_p, y_s, new_ckv, new_krope)
```

```python
import functools
import math

import jax
import jax.numpy as jnp
import numpy as np
from jax import lax
from jax.experimental import pallas as pl
from jax.experimental.pallas import tpu as pltpu

F32 = jnp.float32
BF16 = jnp.bfloat16

D_MODEL = 1024
GRID_W = 64
CHUNK = 128
A_GROUPS = 4
A_WIDTH = 512
MLA_HEADS = 4
QK_NOPE = 128
QK_ROPE = 64
V_DIM = 128
Q_RANK = 384
KV_RANK = 256
ROPE_THETA = 10000.0
N_KEYS = 128
PEER_HEADS = 8
PEER_TOPK = 16
N_EXPERTS = N_KEYS * N_KEYS
DEPTH = 1
ALPHA = (2.0 * DEPTH) ** 0.25
EPS = 1e-6

LANES = 128
HEAD_W = 2 * LANES
IN_EXT = 2 * A_WIDTH + Q_RANK + KV_RANK + 2 * QK_ROPE

PRE_TT = 256
ATT_TQ = 256
POST_TT = 256
ROUTE_GROUP = 2
PEER_TT = 512
PEER_ET = 1024
PEER_QT = 256
VMEM_LIMIT = 56 * 1024 * 1024


def _cparams(sem):
    return pltpu.CompilerParams(dimension_semantics=sem, vmem_limit_bytes=VMEM_LIMIT)


def _gelu(x):
    return x * (lax.erf(x * (1.0 / math.sqrt(2.0))) + 1.0) * 0.5


def _dot(a, b):
    return jnp.dot(a, b, preferred_element_type=F32)


def _mod_kernel(c_ref, w_ref, b_ref, o_ref):
    c = c_ref[...]
    sc = c * (1.0 / (1.0 + jnp.exp(-c)))
    o_ref[...] = _dot(sc.astype(BF16), w_ref[...].astype(BF16)) + b_ref[...]


def _mod_call(c_rows, w_mod, b_mod):
    n = w_mod.shape[1]
    bn = 1536
    return pl.pallas_call(
        _mod_kernel,
        grid=(n // bn,),
        in_specs=[pl.BlockSpec((8, D_MODEL), lambda j: (0, 0)),
                  pl.BlockSpec((D_MODEL, bn), lambda j: (0, j)),
                  pl.BlockSpec((1, bn), lambda j: (0, j))],
        out_specs=pl.BlockSpec((8, bn), lambda j: (0, j)),
        out_shape=jax.ShapeDtypeStruct((8, n), F32),
        compiler_params=_cparams(("arbitrary",)),
        name="mod",
    )(c_rows, w_mod, b_mod)


def _rms(x, g):
    return x * lax.rsqrt(jnp.mean(x * x, axis=-1, keepdims=True) + EPS) * g


def _pre_kernel(x_ref, mod_ref, win_ref, ws_ref, bsb_ref, gq_ref, wqb_ref, gkv_ref, wkvb_ref,
                cos_ref, sin_ref, a_ref, q_ref, k_ref, v_ref, ckv_ref, kr_ref):
    x = x_ref[...]
    shift1 = mod_ref[0, 0:1, :]
    scale1 = mod_ref[0, 1:2, :]
    h = x * (1.0 + scale1) + shift1
    z = _dot(h.astype(BF16), win_ref[...])
    tt = x.shape[0]

    for g in range(A_GROUPS):
        lo = g * LANES
        ug = _gelu(z[:, lo:lo + LANES])
        vg = _gelu(z[:, A_WIDTH + lo:A_WIDTH + lo + LANES])
        mu = jnp.mean(vg, axis=-1, keepdims=True)
        vc = vg - mu
        var = jnp.mean(vc * vc, axis=-1, keepdims=True)
        vs = (vc * lax.rsqrt(var + EPS)).astype(BF16)
        for c in range(tt // CHUNK):
            r0 = c * CHUNK
            mixed = _dot(ws_ref[g], vs[r0:r0 + CHUNK, :]) + bsb_ref[g]
            a_ref[r0:r0 + CHUNK, lo:lo + LANES] = (ug[r0:r0 + CHUNK, :] * mixed).astype(BF16)

    cos = cos_ref[...]
    sin = sin_ref[...]
    o_q = 2 * A_WIDTH
    qn = _rms(z[:, o_q:o_q + Q_RANK], gq_ref[...])
    zq = _dot(qn.astype(BF16), wqb_ref[...])
    for hd in range(MLA_HEADS):
        c0 = hd * HEAD_W
        q_ref[:, c0:c0 + LANES] = zq[:, c0:c0 + LANES].astype(BF16)
        blk = zq[:, c0 + LANES:c0 + HEAD_W]
        q_ref[:, c0 + LANES:c0 + HEAD_W] = (blk * cos + pltpu.roll(blk, 64, 1) * sin).astype(BF16)

    o_kv = o_q + Q_RANK
    ckv = _rms(z[:, o_kv:o_kv + KV_RANK], gkv_ref[...])
    ckv_ref[...] = ckv
    o_kr = o_kv + KV_RANK
    kr_ref[...] = z[:, o_kr:o_kr + QK_ROPE]
    kv = _dot(ckv.astype(BF16), wkvb_ref[...])
    kblk = z[:, o_kr:o_kr + LANES]
    yk = (kblk * cos + pltpu.roll(kblk, 64, 1) * sin).astype(BF16)
    for hd in range(MLA_HEADS):
        c0 = hd * HEAD_W
        k_ref[:, c0:c0 + LANES] = kv[:, c0:c0 + LANES].astype(BF16)
        k_ref[:, c0 + LANES:c0 + HEAD_W] = yk
        v_ref[:, hd * V_DIM:(hd + 1) * V_DIM] = kv[:, c0 + LANES:c0 + HEAD_W].astype(BF16)


def _pre_call(x, mod, mod_row_of_tile, pos_tiles, w, cos_t, sin_t):
    t = x.shape[0]
    tt = PRE_TT
    full = lambda shape: pl.BlockSpec(shape, lambda i: (0,) * len(shape))
    tok = lambda width: pl.BlockSpec((tt, width), lambda i: (i, 0))
    return pl.pallas_call(
        _pre_kernel,
        grid=(t // tt,),
        in_specs=[tok(D_MODEL),
                  pl.BlockSpec((1, 6, D_MODEL), lambda i: (mod_row_of_tile(i), 0, 0)),
                  full((D_MODEL, IN_EXT)), full((A_GROUPS, CHUNK, CHUNK)), full((A_GROUPS, CHUNK, LANES)),
                  full((1, Q_RANK)), full((Q_RANK, MLA_HEADS * HEAD_W)),
                  full((1, KV_RANK)), full((KV_RANK, MLA_HEADS * HEAD_W)),
                  pl.BlockSpec((tt, LANES), lambda i: (i % pos_tiles, 0)),
                  pl.BlockSpec((tt, LANES), lambda i: (i % pos_tiles, 0))],
        out_specs=[tok(A_WIDTH), tok(MLA_HEADS * HEAD_W), tok(MLA_HEADS * HEAD_W), tok(MLA_HEADS * V_DIM),
                   tok(KV_RANK), tok(QK_ROPE)],
        out_shape=[jax.ShapeDtypeStruct((t, A_WIDTH), BF16),
                   jax.ShapeDtypeStruct((t, MLA_HEADS * HEAD_W), BF16),
                   jax.ShapeDtypeStruct((t, MLA_HEADS * HEAD_W), BF16),
                   jax.ShapeDtypeStruct((t, MLA_HEADS * V_DIM), BF16),
                   jax.ShapeDtypeStruct((t, KV_RANK), F32),
                   jax.ShapeDtypeStruct((t, QK_ROPE), F32)],
        compiler_params=_cparams(("parallel",)),
        name="pre",
    )(x, mod, w["w_in"], w["w_s"], w["bsb"], w["g_q"], w["w_qb"], w["g_kv"], w["w_kvb"], cos_t, sin_t)


def _cache_kernel(ckv_ref, krp_ref, wkvb_ref, k_ref, v_ref):
    kv = _dot(ckv_ref[...].astype(BF16), wkvb_ref[...])
    yk = krp_ref[...].astype(BF16)
    for hd in range(MLA_HEADS):
        c0 = hd * HEAD_W
        k_ref[:, c0:c0 + LANES] = kv[:, c0:c0 + LANES].astype(BF16)
        k_ref[:, c0 + LANES:c0 + HEAD_W] = yk
        v_ref[:, hd * V_DIM:(hd + 1) * V_DIM] = kv[:, c0 + LANES:c0 + HEAD_W].astype(BF16)


def _cache_call(ckv, krp, w_kvb):
    t = ckv.shape[0]
    tt = 256
    return pl.pallas_call(
        _cache_kernel,
        grid=(t // tt,),
        in_specs=[pl.BlockSpec((tt, KV_RANK), lambda i: (i, 0)),
                  pl.BlockSpec((tt, LANES), lambda i: (i, 0)),
                  pl.BlockSpec((KV_RANK, MLA_HEADS * HEAD_W), lambda i: (0, 0))],
        out_specs=[pl.BlockSpec((tt, MLA_HEADS * HEAD_W), lambda i: (i, 0)),
                   pl.BlockSpec((tt, MLA_HEADS * V_DIM), lambda i: (i, 0))],
        out_shape=[jax.ShapeDtypeStruct((t, MLA_HEADS * HEAD_W), BF16),
                   jax.ShapeDtypeStruct((t, MLA_HEADS * V_DIM), BF16)],
        compiler_params=_cparams(("parallel",)),
        name="cache_kv",
    )(ckv, krp, w_kvb)


def _attn_kernel(q_ref, k_ref, v_ref, o_ref):
    scale = 1.0 / math.sqrt(QK_NOPE + QK_ROPE)
    for hd in range(MLA_HEADS):
        c0 = hd * HEAD_W
        q = q_ref[0, :, c0:c0 + HEAD_W]
        k = k_ref[0, :, c0:c0 + HEAD_W]
        s = lax.dot_general(q, k, (((1,), (1,)), ((), ())), preferred_element_type=F32)
        m = jnp.max(s, axis=-1, keepdims=True)
        p = jnp.exp2((s - m) * (scale * math.log2(math.e)))
        l = jnp.sum(p, axis=-1, keepdims=True)
        o = _dot(p.astype(BF16), v_ref[0, :, hd * V_DIM:(hd + 1) * V_DIM])
        o_ref[0, :, hd * V_DIM:(hd + 1) * V_DIM] = (o / l).astype(BF16)


def _attn_call(q, k, v):
    b, n, _ = q.shape
    m = k.shape[1]
    tq = ATT_TQ
    return pl.pallas_call(
        _attn_kernel,
        grid=(b, n // tq),
        in_specs=[pl.BlockSpec((1, tq, MLA_HEADS * HEAD_W), lambda i, j: (i, j, 0)),
                  pl.BlockSpec((1, m, MLA_HEADS * HEAD_W), lambda i, j: (i, 0, 0)),
                  pl.BlockSpec((1, m, MLA_HEADS * V_DIM), lambda i, j: (i, 0, 0))],
        out_specs=pl.BlockSpec((1, tq, MLA_HEADS * V_DIM), lambda i, j: (i, j, 0)),
        out_shape=jax.ShapeDtypeStruct((b, n, MLA_HEADS * V_DIM), BF16),
        compiler_params=_cparams(("parallel", "arbitrary")),
        name="attn",
    )(q, k, v)


def _layer_norm(x, g, b):
    mu = jnp.mean(x, axis=-1, keepdims=True)
    xc = x - mu
    var = jnp.mean(xc * xc, axis=-1, keepdims=True)
    return xc * lax.rsqrt(var + EPS) * g + b


def _first_max(x, exact):
    m = jnp.max(x, axis=0, keepdims=True)
    hit = x == m
    if exact:
        iota = lax.broadcasted_iota(jnp.int32, x.shape, 0)
        hit = iota == jnp.min(jnp.where(hit, iota, x.shape[0]), axis=0, keepdims=True)
    return m, hit


def _top16(s, exact, want_rank=True):
    nk, tt = s.shape
    iota16 = lax.broadcasted_iota(jnp.int32, (PEER_TOPK, tt), 0)
    rank = jnp.full((nk, tt), float(PEER_TOPK), F32) if want_rank else None
    vals = jnp.zeros((PEER_TOPK, tt), F32)
    work = s
    for r in range(PEER_TOPK):
        m, hit = _first_max(work, exact)
        if want_rank:
            rank = jnp.where(hit, float(r), rank)
        work = jnp.where(hit, -jnp.inf, work)
        vals = jnp.where(iota16 == r, m, vals)
    return vals, rank


def _young_counts(v1, v2, exact):
    n = jnp.zeros(v1.shape, F32)
    f = v1 + v2[0:1, :]
    for r in range(PEER_TOPK):
        _, hit = _first_max(f, exact)
        n = jnp.where(hit, n + 1.0, n)
        if r + 1 < PEER_TOPK:
            taken = jnp.sum(jnp.where(hit, n, 0.0), axis=0, keepdims=True)
            nxt = jnp.full(taken.shape, -jnp.inf, F32)
            for b in range(1, PEER_TOPK):
                nxt = jnp.where(taken == float(b), v2[b:b + 1, :], nxt)
            f = jnp.where(hit, v1 + nxt, f)
    return n


def _route_head(s1, s2, exact):
    v1, rk1 = _top16(s1, exact, want_rank=exact)
    v2, rk2 = _top16(s2, exact)
    n = _young_counts(v1, v2, exact)
    e1t = jnp.exp(v1 - v1[0:1, :])
    e2t = jnp.exp(v2 - v2[0:1, :])
    za = jnp.zeros_like(e1t)
    for b in range(PEER_TOPK):
        za = za + jnp.where(n > float(b), e2t[b:b + 1, :], 0.0)
    inv_z = 1.0 / jnp.sum(e1t * za, axis=0, keepdims=True)
    n1 = jnp.zeros_like(s1)
    for a in range(PEER_TOPK):
        is_a = (rk1 == float(a)) if exact else (s1 == v1[a:a + 1, :])
        n1 = jnp.where(is_a, n[a:a + 1, :], n1)
    e2 = jnp.exp(s2 - v2[0:1, :])
    e1z = jnp.exp(s1 - v1[0:1, :]) * inv_z
    count = lambda mask: jnp.sum(jnp.where(mask, 1.0, 0.0), axis=0, keepdims=True)
    k = float(PEER_TOPK)
    picked1 = count(s1 >= v1[PEER_TOPK - 1:PEER_TOPK, :])
    tie = (picked1 != k) | (count(rk2 < k) != k) | (jnp.sum(n, axis=0, keepdims=True) != k)
    return rk2, e2, n1, e1z, tie


def _post_kernel(x_ref, a_ref, b_ref, mod_ref, wo_ref, g1_ref, b1_ref, wpq_ref, sk_ref,
                 x1_ref, h2t_ref, r2_ref, e2_ref, n1_ref, e1z_ref, qt_scr, s_scr):
    gate1 = mod_ref[0, 2:3, :]
    shift2 = mod_ref[0, 3:4, :]
    scale2 = mod_ref[0, 4:5, :]
    mix = _dot(a_ref[...], wo_ref[0:A_WIDTH, :]) + _dot(b_ref[...], wo_ref[A_WIDTH:2 * A_WIDTH, :])
    x1 = _layer_norm(ALPHA * x_ref[...] + gate1 * mix, g1_ref[...], b1_ref[...])
    x1_ref[...] = x1
    h2 = x1 * (1.0 + scale2) + shift2
    h2t = h2.T.astype(BF16)
    h2t_ref[...] = h2t
    qt_scr[...] = _dot(wpq_ref[...], h2t)

    def group_body(g, carry):
        heads = [g * ROUTE_GROUP + i for i in range(ROUTE_GROUP)]
        for i, h in enumerate(heads):
            r0 = pl.multiple_of(h * 2 * N_KEYS, 2 * N_KEYS)
            q1 = qt_scr[pl.ds(r0, N_KEYS), :].astype(BF16)
            q2 = qt_scr[pl.ds(r0 + N_KEYS, N_KEYS), :].astype(BF16)
            s_scr[2 * i] = _dot(sk_ref[2 * h], q1)
            s_scr[2 * i + 1] = _dot(sk_ref[2 * h + 1], q2)

        def emit(i, h, exact):
            rk2, e2, n1, e1z, tie = _route_head(s_scr[2 * i], s_scr[2 * i + 1], exact)
            r2_ref[h] = rk2.astype(BF16)
            e2_ref[h] = e2.astype(BF16)
            n1_ref[h] = n1
            e1z_ref[h] = e1z
            return jnp.max(jnp.where(tie, 1.0, 0.0))

        ties = [emit(i, h, False) for i, h in enumerate(heads)]
        for i, h in enumerate(heads):
            @pl.when(ties[i] > 0.0)
            def _(i=i, h=h):
                emit(i, h, True)

        return carry

    lax.fori_loop(0, PEER_HEADS // ROUTE_GROUP, group_body, 0)


def _post_call(x, a_out, b_out, mod, mod_row_of_tile, w):
    t = x.shape[0]
    tt = POST_TT
    full = lambda shape: pl.BlockSpec(shape, lambda i: (0,) * len(shape))
    tok = lambda width: pl.BlockSpec((tt, width), lambda i: (i, 0))
    rout = pl.BlockSpec((PEER_HEADS, N_KEYS, tt), lambda i: (0, 0, i))
    return pl.pallas_call(
        _post_kernel,
        grid=(t // tt,),
        in_specs=[tok(D_MODEL), tok(A_WIDTH), tok(A_WIDTH),
                  pl.BlockSpec((1, 6, D_MODEL), lambda i: (mod_row_of_tile(i), 0, 0)),
                  full((D_MODEL, D_MODEL)), full((1, D_MODEL)), full((1, D_MODEL)),
                  full((PEER_HEADS * 2 * N_KEYS, D_MODEL)), full((PEER_HEADS * 2, N_KEYS, N_KEYS))],
        out_specs=[tok(D_MODEL), pl.BlockSpec((D_MODEL, tt), lambda i: (0, i)), rout, rout, rout, rout],
        out_shape=[jax.ShapeDtypeStruct((t, D_MODEL), F32),
                   jax.ShapeDtypeStruct((D_MODEL, t), BF16),
                   jax.ShapeDtypeStruct((PEER_HEADS, N_KEYS, t), BF16),
                   jax.ShapeDtypeStruct((PEER_HEADS, N_KEYS, t), BF16),
                   jax.ShapeDtypeStruct((PEER_HEADS, N_KEYS, t), F32),
                   jax.ShapeDtypeStruct((PEER_HEADS, N_KEYS, t), F32)],
        scratch_shapes=[pltpu.VMEM((PEER_HEADS * 2 * N_KEYS, tt), F32),
                        pltpu.VMEM((2 * ROUTE_GROUP, N_KEYS, tt), F32)],
        compiler_params=_cparams(("parallel",)),
        name="post_route",
    )(x, a_out, b_out, mod, w["w_o"], w["ln1_g"], w["ln1_b"], w["w_pqT"], w["sub_keys"])


BF16_ROWS = 16


def _peer_phases(h2t_ref, r2_ref, e2_ref, n1_ref, e1z_ref, u_ref, vt_ref, a_w, a_r, p_w, p_r, acc_scr, valid):
    n_c = h2t_ref.shape[1] // LANES
    n_slab = N_KEYS // BF16_ROWS
    bcast16 = lambda row: jnp.broadcast_to(row, (BF16_ROWS, LANES)).astype(BF16)
    act = _gelu(_dot(u_ref[...], h2t_ref[...])).astype(BF16)
    for c in range(n_c):
        a_w[c] = act[:, c * LANES:(c + 1) * LANES]
    p_all = jnp.concatenate([p_r[c] for c in range(n_c)], axis=1)
    acc_scr[...] += _dot(vt_ref[...], p_all)

    for q0 in range(0, u_ref.shape[0], PEER_QT):
        ils = range(q0 // N_KEYS, (q0 + PEER_QT) // N_KEYS)
        for c in range(n_c):
            l0 = c * LANES
            wgt = {il: [None] * n_slab for il in ils}
            for h in range(PEER_HEADS):
                rows = {il: (bcast16(n1_ref[h, il:il + 1, l0:l0 + LANES] * valid),
                             bcast16(e1z_ref[h, il:il + 1, l0:l0 + LANES])) for il in ils}
                for k in range(n_slab):
                    k0 = k * BF16_ROWS
                    r2s = r2_ref[h, c, k0:k0 + BF16_ROWS, :]
                    e2s = e2_ref[h, c, k0:k0 + BF16_ROWS, :]
                    for il in ils:
                        n16, e16 = rows[il]
                        term = jnp.where(r2s < n16, e2s, jnp.zeros((), BF16)) * e16
                        wgt[il][k] = term if wgt[il][k] is None else wgt[il][k] + term
            for il in ils:
                for k in range(n_slab):
                    k0 = il * N_KEYS + k * BF16_ROWS
                    p_w[c, k0:k0 + BF16_ROWS, :] = a_r[c, k0:k0 + BF16_ROWS, :] * wgt[il][k]


def _peer_kernel(h2t_ref, r2_ref, e2_ref, n1_ref, e1z_ref, u_ref, vt_ref, x1_ref, mod_ref, g2_ref, b2_ref,
                 y_ref, a0, a1, p0, p1, acc_scr, r2_scr, e2_scr, *, n_e, n_blocks):
    s = pl.program_id(0)

    @pl.when(s == 0)
    def _():
        a1[...] = jnp.zeros_like(a1)
        p0[...] = jnp.zeros_like(p0)
        p1[...] = jnp.zeros_like(p1)

    @pl.when(jnp.clip(s - 2, 0, n_blocks - 1) % n_e == 0)
    def _():
        acc_scr[...] = jnp.zeros_like(acc_scr)

    @pl.when(jnp.clip(s - 1, 0, n_blocks - 1) % n_e == 0)
    def _():
        for c in range(r2_scr.shape[1]):
            r2_scr[:, c] = r2_ref[:, :, c * LANES:(c + 1) * LANES]
            e2_scr[:, c] = e2_ref[:, :, c * LANES:(c + 1) * LANES]

    valid = jnp.where((s >= 1) & (s <= n_blocks), 1.0, 0.0).astype(F32)
    args = (h2t_ref, r2_scr, e2_scr, n1_ref, e1z_ref, u_ref, vt_ref)

    @pl.when(s % 2 == 0)
    def _():
        _peer_phases(*args, a0, a1, p1, p0, acc_scr, valid)

    @pl.when(s % 2 == 1)
    def _():
        _peer_phases(*args, a1, a0, p0, p1, acc_scr, valid)

    @pl.when((s >= 2) & ((s - 2) % n_e == n_e - 1))
    def _():
        gate2 = mod_ref[0, 5:6, :]
        peer = acc_scr[...].T
        y_ref[...] = _layer_norm(ALPHA * x1_ref[...] + gate2 * peer, g2_ref[...], b2_ref[...])


def _peer_call(h2t, r2, e2, n1, e1z, x1, mod, mod_row_of_tile, w):
    t = x1.shape[0]
    tt, et = PEER_TT, PEER_ET
    n_i = et // N_KEYS
    n_e = N_EXPERTS // et
    n_blocks = (t // tt) * n_e
    n_c = tt // LANES
    blk = lambda s, lag: jnp.clip(s - lag, 0, n_blocks - 1)
    tile = lambda s, lag: blk(s, lag) // n_e
    exp = lambda s, lag: blk(s, lag) % n_e
    rfull = pl.BlockSpec((PEER_HEADS, N_KEYS, tt), lambda s: (0, 0, tile(s, 1)))
    rrow = pl.BlockSpec((PEER_HEADS, n_i, tt), lambda s: (0, exp(s, 1), tile(s, 1)))
    return pl.pallas_call(
        functools.partial(_peer_kernel, n_e=n_e, n_blocks=n_blocks),
        grid=(n_blocks + 2,),
        in_specs=[pl.BlockSpec((D_MODEL, tt), lambda s: (0, tile(s, 0))),
                  rfull, rfull, rrow, rrow,
                  pl.BlockSpec((et, D_MODEL), lambda s: (exp(s, 0), 0)),
                  pl.BlockSpec((D_MODEL, et), lambda s: (0, exp(s, 2))),
                  pl.BlockSpec((tt, D_MODEL), lambda s: (tile(s, 2), 0)),
                  pl.BlockSpec((1, 6, D_MODEL), lambda s: (mod_row_of_tile(tile(s, 2)), 0, 0)),
                  pl.BlockSpec((1, D_MODEL), lambda s: (0, 0)),
                  pl.BlockSpec((1, D_MODEL), lambda s: (0, 0))],
        out_specs=pl.BlockSpec((tt, D_MODEL), lambda s: (tile(s, 2), 0)),
        out_shape=jax.ShapeDtypeStruct((t, D_MODEL), F32),
        scratch_shapes=[pltpu.VMEM((n_c, et, LANES), BF16), pltpu.VMEM((n_c, et, LANES), BF16),
                        pltpu.VMEM((n_c, et, LANES), BF16), pltpu.VMEM((n_c, et, LANES), BF16),
                        pltpu.VMEM((D_MODEL, tt), F32),
                        pltpu.VMEM((PEER_HEADS, n_c, N_KEYS, LANES), BF16),
                        pltpu.VMEM((PEER_HEADS, n_c, N_KEYS, LANES), BF16)],
        compiler_params=_cparams(("arbitrary",)),
        name="peer_dense",
    )(h2t, r2, e2, n1, e1z, w["u_tab"], w["v_tabT"], x1, mod, w["ln2_g"], w["ln2_b"])


_SWAP64 = np.concatenate([np.arange(16, 32), np.arange(0, 16), np.arange(48, 64), np.arange(32, 48)])


def _rope_tables(n_pos):
    pos = np.arange(n_pos)
    row = (pos // GRID_W).astype(np.float32)
    col = (pos % GRID_W).astype(np.float32)
    f = QK_ROPE // 4
    freqs = (1.0 / (np.float32(ROPE_THETA) ** (np.arange(f, dtype=np.float32) / np.float32(f)))).astype(np.float32)
    ar = row[:, None] * freqs[None, :]
    ac = col[:, None] * freqs[None, :]
    zeros = np.zeros((n_pos, QK_ROPE), np.float32)
    cos_t = np.concatenate([np.cos(ar), np.cos(ar), np.cos(ac), np.cos(ac), zeros], axis=1)
    sin_t = np.concatenate([-np.sin(ar), np.sin(ar), -np.sin(ac), np.sin(ac), zeros], axis=1)
    return jnp.asarray(cos_t, F32), jnp.asarray(sin_t, F32)


def _identity_tables(n_pos):
    ones = np.ones((n_pos, QK_ROPE), np.float32)
    zeros = np.zeros((n_pos, QK_ROPE), np.float32)
    return jnp.asarray(np.concatenate([ones, zeros], axis=1)), jnp.zeros((n_pos, LANES), F32)


def _prep_weights(w_in, w_s, b_s, g_q, w_qb, g_kv, w_kvb, w_o, ln1_g, ln1_b, w_pq, sub_keys, u_tab, v_tab,
                  ln2_g, ln2_b):
    o_kr = 2 * A_WIDTH + Q_RANK + KV_RANK
    w_in_ext = jnp.concatenate([w_in, w_in[:, o_kr + _SWAP64]], axis=1).astype(BF16)
    wq = w_qb.reshape(Q_RANK, MLA_HEADS, QK_NOPE + QK_ROPE)
    w_qb_ext = jnp.concatenate([wq, wq[:, :, QK_NOPE + _SWAP64]], axis=2).reshape(Q_RANK, MLA_HEADS * HEAD_W)
    return {
        "w_in": w_in_ext,
        "w_s": w_s.astype(BF16),
        "bsb": jnp.broadcast_to(b_s[:, :, None], (A_GROUPS, CHUNK, LANES)),
        "g_q": g_q[None, :],
        "w_qb": w_qb_ext.astype(BF16),
        "g_kv": g_kv[None, :],
        "w_kvb": w_kvb.astype(BF16),
        "w_o": w_o.astype(BF16),
        "ln1_g": ln1_g[None, :],
        "ln1_b": ln1_b[None, :],
        "w_pqT": w_pq.T.astype(BF16),
        "sub_keys": sub_keys.reshape(PEER_HEADS * 2, N_KEYS, N_KEYS).astype(BF16),
        "u_tab": u_tab.astype(BF16),
        "v_tabT": v_tab.T.astype(BF16),
        "ln2_g": ln2_g[None, :],
        "ln2_b": ln2_b[None, :],
    }


def _trunk(x, mod, mod_row_of_seq, w, cos_t, sin_t, cache):
    b, n, _ = x.shape
    xf = x.reshape(b * n, D_MODEL)
    row_of = lambda tile_tokens: (lambda i: mod_row_of_seq(i // (n // tile_tokens)))
    a_out, q, k, v, ckv, kr = _pre_call(xf, mod, row_of(PRE_TT), n // PRE_TT, w, cos_t, sin_t)
    q = q.reshape(b, n, -1)
    k = k.reshape(b, n, -1)
    v = v.reshape(b, n, -1)
    if cache is not None:
        k = jnp.concatenate([cache[0], k], axis=1)
        v = jnp.concatenate([cache[1], v], axis=1)
    b_out = _attn_call(q, k, v).reshape(b * n, -1)
    x1, h2t, r2, e2, n1, e1z = _post_call(xf, a_out, b_out, mod, row_of(POST_TT), w)
    y = _peer_call(h2t, r2, e2, n1, e1z, x1, mod, row_of(PEER_TT), w)
    return y.reshape(b, n, D_MODEL), ckv, kr


def kernel(x_prompt, x_sample, cache_ckv, cache_krope, c, c_ctx, w_mod, b_mod, w_in, w_s, b_s, g_q, w_qb, g_kv,
           w_kvb, w_o, ln1_g, ln1_b, w_pq, sub_keys, u_tab, v_tab, ln2_g, ln2_b):
    batch, seq, _ = x_prompt.shape
    dec_batch, dec_seq, _ = x_sample.shape
    past = cache_ckv.shape[2]
    y_p, y_s = x_prompt, x_sample
    ckv_list, kr_list = [], []
    for l in range(DEPTH):
        w = _prep_weights(w_in[l], w_s[l], b_s[l], g_q[l], w_qb[l], g_kv[l], w_kvb[l], w_o[l], ln1_g[l], ln1_b[l],
                          w_pq[l], sub_keys[l], u_tab[l], v_tab[l], ln2_g[l], ln2_b[l])
        c_rows = jnp.concatenate([c_ctx[None, :], c, jnp.zeros((8 - 1 - dec_batch, D_MODEL), F32)], axis=0)
        mod = _mod_call(c_rows, w_mod[l], b_mod[l][None, :]).reshape(8, 6, D_MODEL)

        cos_c, sin_c = _identity_tables(seq)
        y_p, ckv_l, kr_l = _trunk(y_p, mod, lambda s: 0, w, cos_c, sin_c, None)
        ckv_list.append(ckv_l.reshape(batch, seq, KV_RANK))
        kr_list.append(kr_l.reshape(batch, seq, QK_ROPE))

        krp = jnp.pad(cache_krope[:, l].reshape(dec_batch * past, QK_ROPE), ((0, 0), (0, LANES - QK_ROPE)))
        k_c, v_c = _cache_call(cache_ckv[:, l].reshape(dec_batch * past, KV_RANK), krp, w["w_kvb"])
        cache = (k_c.reshape(dec_batch, past, -1), v_c.reshape(dec_batch, past, -1))
        cos_l, sin_l = _rope_tables(dec_seq)
        y_s, _, _ = _trunk(y_s, mod, lambda s: 1 + s, w, cos_l, sin_l, cache)
    new_ckv = jnp.stack(ckv_list, axis=1)
    new_krope = jnp.stack(kr_list, axis=1)
    return (y_p, y_s, new_ckv, new_krope)
```

```python
import functools
import math

import jax
import jax.numpy as jnp
import numpy as np
from jax import lax
from jax.experimental import pallas as pl
from jax.experimental.pallas import tpu as pltpu

F32 = jnp.float32
BF16 = jnp.bfloat16

D_MODEL = 1024
GRID_W = 64
CHUNK = 128
A_GROUPS = 4
A_WIDTH = 512
MLA_HEADS = 4
QK_NOPE = 128
QK_ROPE = 64
V_DIM = 128
Q_RANK = 384
KV_RANK = 256
ROPE_THETA = 10000.0
N_KEYS = 128
PEER_HEADS = 8
PEER_TOPK = 16
N_EXPERTS = N_KEYS * N_KEYS
DEPTH = 1
ALPHA = (2.0 * DEPTH) ** 0.25
EPS = 1e-6

LANES = 128
SUBLANES = 8
HEAD_W = 2 * LANES
IN_EXT = 2 * A_WIDTH + Q_RANK + KV_RANK + 2 * QK_ROPE

PRE_TT = 256
ATT_TQ = 256
POST_TT = 256
ROUTE_GROUP = SUBLANES * LANES // POST_TT
PEER_TT = 512
PEER_ET = 1024
PEER_QT = 256
VMEM_LIMIT = 56 * 1024 * 1024


def _cparams(sem):
    return pltpu.CompilerParams(dimension_semantics=sem, vmem_limit_bytes=VMEM_LIMIT)


def _gelu(x):
    return x * (lax.erf(x * (1.0 / math.sqrt(2.0))) + 1.0) * 0.5


def _dot(a, b):
    return jnp.dot(a, b, preferred_element_type=F32)


def _mod_kernel(c_ref, w_ref, b_ref, o_ref):
    c = c_ref[...]
    sc = c * (1.0 / (1.0 + jnp.exp(-c)))
    o_ref[...] = _dot(sc.astype(BF16), w_ref[...].astype(BF16)) + b_ref[...]


def _mod_call(c_rows, w_mod, b_mod):
    n = w_mod.shape[1]
    bn = 1536
    return pl.pallas_call(
        _mod_kernel,
        grid=(n // bn,),
        in_specs=[pl.BlockSpec((8, D_MODEL), lambda j: (0, 0)),
                  pl.BlockSpec((D_MODEL, bn), lambda j: (0, j)),
                  pl.BlockSpec((1, bn), lambda j: (0, j))],
        out_specs=pl.BlockSpec((8, bn), lambda j: (0, j)),
        out_shape=jax.ShapeDtypeStruct((8, n), F32),
        compiler_params=_cparams(("arbitrary",)),
        name="mod",
    )(c_rows, w_mod, b_mod)


def _rms(x, g):
    return x * lax.rsqrt(jnp.mean(x * x, axis=-1, keepdims=True) + EPS) * g


def _pre_kernel(x_ref, mod_ref, win_ref, ws_ref, bsb_ref, gq_ref, wqb_ref, gkv_ref, wkvb_ref,
                cos_ref, sin_ref, a_ref, q_ref, k_ref, v_ref, ckv_ref, kr_ref):
    x = x_ref[...]
    shift1 = mod_ref[0, 0:1, :]
    scale1 = mod_ref[0, 1:2, :]
    h = x * (1.0 + scale1) + shift1
    z = _dot(h.astype(BF16), win_ref[...])
    tt = x.shape[0]

    for g in range(A_GROUPS):
        lo = g * LANES
        ug = _gelu(z[:, lo:lo + LANES])
        vg = _gelu(z[:, A_WIDTH + lo:A_WIDTH + lo + LANES])
        mu = jnp.mean(vg, axis=-1, keepdims=True)
        vc = vg - mu
        var = jnp.mean(vc * vc, axis=-1, keepdims=True)
        vs = (vc * lax.rsqrt(var + EPS)).astype(BF16)
        for c in range(tt // CHUNK):
            r0 = c * CHUNK
            mixed = _dot(ws_ref[g], vs[r0:r0 + CHUNK, :]) + bsb_ref[g]
            a_ref[r0:r0 + CHUNK, lo:lo + LANES] = (ug[r0:r0 + CHUNK, :] * mixed).astype(BF16)

    cos = cos_ref[...]
    sin = sin_ref[...]
    o_q = 2 * A_WIDTH
    qn = _rms(z[:, o_q:o_q + Q_RANK], gq_ref[...])
    zq = _dot(qn.astype(BF16), wqb_ref[...])
    for hd in range(MLA_HEADS):
        c0 = hd * HEAD_W
        q_ref[:, c0:c0 + LANES] = zq[:, c0:c0 + LANES].astype(BF16)
        blk = zq[:, c0 + LANES:c0 + HEAD_W]
        q_ref[:, c0 + LANES:c0 + HEAD_W] = (blk * cos + pltpu.roll(blk, 64, 1) * sin).astype(BF16)

    o_kv = o_q + Q_RANK
    ckv = _rms(z[:, o_kv:o_kv + KV_RANK], gkv_ref[...])
    ckv_ref[...] = ckv
    o_kr = o_kv + KV_RANK
    kr_ref[...] = z[:, o_kr:o_kr + QK_ROPE]
    kv = _dot(ckv.astype(BF16), wkvb_ref[...])
    kblk = z[:, o_kr:o_kr + LANES]
    yk = (kblk * cos + pltpu.roll(kblk, 64, 1) * sin).astype(BF16)
    for hd in range(MLA_HEADS):
        c0 = hd * HEAD_W
        k_ref[:, c0:c0 + LANES] = kv[:, c0:c0 + LANES].astype(BF16)
        k_ref[:, c0 + LANES:c0 + HEAD_W] = yk
        v_ref[:, hd * V_DIM:(hd + 1) * V_DIM] = kv[:, c0 + LANES:c0 + HEAD_W].astype(BF16)


def _pre_call(x, mod, mod_row_of_tile, pos_tiles, w, cos_t, sin_t):
    t = x.shape[0]
    tt = PRE_TT
    full = lambda shape: pl.BlockSpec(shape, lambda i: (0,) * len(shape))
    tok = lambda width: pl.BlockSpec((tt, width), lambda i: (i, 0))
    return pl.pallas_call(
        _pre_kernel,
        grid=(t // tt,),
        in_specs=[tok(D_MODEL),
                  pl.BlockSpec((1, 6, D_MODEL), lambda i: (mod_row_of_tile(i), 0, 0)),
                  full((D_MODEL, IN_EXT)), full((A_GROUPS, CHUNK, CHUNK)), full((A_GROUPS, CHUNK, LANES)),
                  full((1, Q_RANK)), full((Q_RANK, MLA_HEADS * HEAD_W)),
                  full((1, KV_RANK)), full((KV_RANK, MLA_HEADS * HEAD_W)),
                  pl.BlockSpec((tt, LANES), lambda i: (i % pos_tiles, 0)),
                  pl.BlockSpec((tt, LANES), lambda i: (i % pos_tiles, 0))],
        out_specs=[tok(A_WIDTH), tok(MLA_HEADS * HEAD_W), tok(MLA_HEADS * HEAD_W), tok(MLA_HEADS * V_DIM),
                   tok(KV_RANK), tok(QK_ROPE)],
        out_shape=[jax.ShapeDtypeStruct((t, A_WIDTH), BF16),
                   jax.ShapeDtypeStruct((t, MLA_HEADS * HEAD_W), BF16),
                   jax.ShapeDtypeStruct((t, MLA_HEADS * HEAD_W), BF16),
                   jax.ShapeDtypeStruct((t, MLA_HEADS * V_DIM), BF16),
                   jax.ShapeDtypeStruct((t, KV_RANK), F32),
                   jax.ShapeDtypeStruct((t, QK_ROPE), F32)],
        compiler_params=_cparams(("parallel",)),
        name="pre",
    )(x, mod, w["w_in"], w["w_s"], w["bsb"], w["g_q"], w["w_qb"], w["g_kv"], w["w_kvb"], cos_t, sin_t)


def _cache_kernel(ckv_ref, krp_ref, wkvb_ref, k_ref, v_ref):
    kv = _dot(ckv_ref[...].astype(BF16), wkvb_ref[...])
    yk = krp_ref[...].astype(BF16)
    for hd in range(MLA_HEADS):
        c0 = hd * HEAD_W
        k_ref[:, c0:c0 + LANES] = kv[:, c0:c0 + LANES].astype(BF16)
        k_ref[:, c0 + LANES:c0 + HEAD_W] = yk
        v_ref[:, hd * V_DIM:(hd + 1) * V_DIM] = kv[:, c0 + LANES:c0 + HEAD_W].astype(BF16)


def _cache_call(ckv, krp, w_kvb):
    t = ckv.shape[0]
    tt = 256
    return pl.pallas_call(
        _cache_kernel,
        grid=(t // tt,),
        in_specs=[pl.BlockSpec((tt, KV_RANK), lambda i: (i, 0)),
                  pl.BlockSpec((tt, LANES), lambda i: (i, 0)),
                  pl.BlockSpec((KV_RANK, MLA_HEADS * HEAD_W), lambda i: (0, 0))],
        out_specs=[pl.BlockSpec((tt, MLA_HEADS * HEAD_W), lambda i: (i, 0)),
                   pl.BlockSpec((tt, MLA_HEADS * V_DIM), lambda i: (i, 0))],
        out_shape=[jax.ShapeDtypeStruct((t, MLA_HEADS * HEAD_W), BF16),
                   jax.ShapeDtypeStruct((t, MLA_HEADS * V_DIM), BF16)],
        compiler_params=_cparams(("parallel",)),
        name="cache_kv",
    )(ckv, krp, w_kvb)


def _attn_kernel(q_ref, k_ref, v_ref, o_ref):
    scale = 1.0 / math.sqrt(QK_NOPE + QK_ROPE)
    for hd in range(MLA_HEADS):
        c0 = hd * HEAD_W
        q = q_ref[0, :, c0:c0 + HEAD_W]
        k = k_ref[0, :, c0:c0 + HEAD_W]
        s = lax.dot_general(q, k, (((1,), (1,)), ((), ())), preferred_element_type=F32)
        m = jnp.max(s, axis=-1, keepdims=True)
        p = jnp.exp2((s - m) * (scale * math.log2(math.e)))
        l = jnp.sum(p, axis=-1, keepdims=True)
        o = _dot(p.astype(BF16), v_ref[0, :, hd * V_DIM:(hd + 1) * V_DIM])
        o_ref[0, :, hd * V_DIM:(hd + 1) * V_DIM] = (o / l).astype(BF16)


def _attn_call(q, k, v):
    b, n, _ = q.shape
    m = k.shape[1]
    tq = ATT_TQ
    return pl.pallas_call(
        _attn_kernel,
        grid=(b, n // tq),
        in_specs=[pl.BlockSpec((1, tq, MLA_HEADS * HEAD_W), lambda i, j: (i, j, 0)),
                  pl.BlockSpec((1, m, MLA_HEADS * HEAD_W), lambda i, j: (i, 0, 0)),
                  pl.BlockSpec((1, m, MLA_HEADS * V_DIM), lambda i, j: (i, 0, 0))],
        out_specs=pl.BlockSpec((1, tq, MLA_HEADS * V_DIM), lambda i, j: (i, j, 0)),
        out_shape=jax.ShapeDtypeStruct((b, n, MLA_HEADS * V_DIM), BF16),
        compiler_params=_cparams(("parallel", "arbitrary")),
        name="attn",
    )(q, k, v)


def _layer_norm(x, g, b):
    mu = jnp.mean(x, axis=-1, keepdims=True)
    xc = x - mu
    var = jnp.mean(xc * xc, axis=-1, keepdims=True)
    return xc * lax.rsqrt(var + EPS) * g + b


def _first_max(x, exact):
    m = jnp.max(x, axis=0, keepdims=True)
    hit = x == m
    if exact:
        iota = lax.broadcasted_iota(jnp.int32, x.shape, 0)
        hit = iota == jnp.min(jnp.where(hit, iota, x.shape[0]), axis=0, keepdims=True)
    return m, hit


def _top16(s, exact, want_rank=True, on_value=None):
    nk, tt = s.shape
    iota16 = lax.broadcasted_iota(jnp.int32, (PEER_TOPK, tt), 0)
    rank = jnp.full((nk, tt), float(PEER_TOPK), F32) if want_rank else None
    vals = jnp.zeros((PEER_TOPK, tt), F32) if on_value is None else None
    work = s
    for r in range(PEER_TOPK):
        m, hit = _first_max(work, exact)
        if want_rank:
            rank = jnp.where(hit, float(r), rank)
        work = jnp.where(hit, -jnp.inf, work)
        if on_value is None:
            vals = jnp.where(iota16 == r, m, vals)
        else:
            on_value(r, m)
    return vals, rank


def _young_counts(v1, v2, exact):
    n = jnp.zeros(v1.shape, F32)
    f = v1 + v2[0:1]
    for r in range(PEER_TOPK):
        _, hit = _first_max(f, exact)
        n = jnp.where(hit, n + 1.0, n)
        if r + 1 < PEER_TOPK:
            taken = jnp.sum(jnp.where(hit, n, 0.0), axis=0, keepdims=True)
            nxt = jnp.full(taken.shape, -jnp.inf, F32)
            for b in range(1, PEER_TOPK):
                nxt = jnp.where(taken == float(b), v2[b:b + 1], nxt)
            f = jnp.where(hit, v1 + nxt, f)
    return n


def _pair_weights(v1, v2, exact):
    n = _young_counts(v1, v2, exact)
    e1t = jnp.exp(v1 - v1[0:1])
    e2t = jnp.exp(v2 - v2[0:1])
    za = jnp.zeros_like(e1t)
    for b in range(PEER_TOPK):
        za = za + jnp.where(n > float(b), e2t[b:b + 1], 0.0)
    inv_z = 1.0 / jnp.sum(e1t * za, axis=0, keepdims=True)
    return n, inv_z


def _count(mask):
    return jnp.sum(jnp.where(mask, 1.0, 0.0), axis=0, keepdims=True)


def _route_head(s1, s2, exact):
    v1, rk1 = _top16(s1, exact, want_rank=exact)
    v2, rk2 = _top16(s2, exact)
    n, inv_z = _pair_weights(v1, v2, exact)
    n1 = jnp.zeros_like(s1)
    for a in range(PEER_TOPK):
        is_a = (rk1 == float(a)) if exact else (s1 == v1[a:a + 1, :])
        n1 = jnp.where(is_a, n[a:a + 1, :], n1)
    e2 = jnp.exp(s2 - v2[0:1, :])
    e1z = jnp.exp(s1 - v1[0:1, :]) * inv_z
    k = float(PEER_TOPK)
    picked1 = _count(s1 >= v1[PEER_TOPK - 1:PEER_TOPK, :])
    tie = (picked1 != k) | (_count(rk2 < k) != k) | (jnp.sum(n, axis=0, keepdims=True) != k)
    return rk2, e2, n1, e1z, tie


def _route_group_fast(s_scr, pk_scr, heads, r2_ref, e2_ref, n1_ref, e1z_ref):
    tt = s_scr.shape[2]
    n_ch = tt // LANES
    k = float(PEER_TOPK)
    chunks = lambda row: [row[:, c * LANES:(c + 1) * LANES] for c in range(n_ch)]
    unpack = lambda slot, a, p0: jnp.concatenate(
        [pk_scr[slot, a, p0 + c:p0 + c + 1, :] for c in range(n_ch)], axis=1)
    ties = []
    first = []
    for i, h in enumerate(heads):
        p0 = i * n_ch
        ends = {}

        def pack(slot):
            def on_value(r, m):
                for c, piece in enumerate(chunks(m)):
                    pk_scr[slot, r, p0 + c:p0 + c + 1, :] = piece
                if r in (0, PEER_TOPK - 1):
                    ends[(slot, r)] = m
            return on_value

        s1 = s_scr[2 * i]
        s2 = s_scr[2 * i + 1]
        _top16(s1, False, want_rank=False, on_value=pack(0))
        _, rk2 = _top16(s2, False, on_value=pack(1))
        r2_ref[h] = rk2.astype(BF16)
        e2_ref[h] = jnp.exp(s2 - ends[(1, 0)]).astype(BF16)
        bad = (_count(s1 >= ends[(0, PEER_TOPK - 1)]) != k) | (_count(rk2 < k) != k)
        ties.append(jnp.max(jnp.where(bad, 1.0, 0.0)))
        first.append(ends[(0, 0)])

    n, inv_z = _pair_weights(pk_scr[0], pk_scr[1], False)
    pk_scr[2] = n
    pk_scr[3, 0] = inv_z[0]
    bad_n = jnp.where(jnp.sum(n, axis=0) != k, 1.0, 0.0)
    ties = [jnp.maximum(t, jnp.max(bad_n[i * n_ch:(i + 1) * n_ch, :])) for i, t in enumerate(ties)]

    for i, h in enumerate(heads):
        p0 = i * n_ch
        s1 = s_scr[2 * i]
        n1 = jnp.zeros_like(s1)
        for a in range(PEER_TOPK):
            n1 = jnp.where(s1 == unpack(0, a, p0), unpack(2, a, p0), n1)
        n1_ref[h] = n1
        e1z_ref[h] = jnp.exp(s1 - first[i]) * unpack(3, 0, p0)
    return ties


def _post_kernel(x_ref, a_ref, b_ref, mod_ref, wo_ref, g1_ref, b1_ref, wpq_ref, sk_ref,
                 x1_ref, h2t_ref, r2_ref, e2_ref, n1_ref, e1z_ref, qt_scr, s_scr, pk_scr):
    gate1 = mod_ref[0, 2:3, :]
    shift2 = mod_ref[0, 3:4, :]
    scale2 = mod_ref[0, 4:5, :]
    mix = _dot(a_ref[...], wo_ref[0:A_WIDTH, :]) + _dot(b_ref[...], wo_ref[A_WIDTH:2 * A_WIDTH, :])
    x1 = _layer_norm(ALPHA * x_ref[...] + gate1 * mix, g1_ref[...], b1_ref[...])
    x1_ref[...] = x1
    h2 = x1 * (1.0 + scale2) + shift2
    h2t = h2.T.astype(BF16)
    h2t_ref[...] = h2t
    qt_scr[...] = _dot(wpq_ref[...], h2t)

    def group_body(g, carry):
        heads = [g * ROUTE_GROUP + i for i in range(ROUTE_GROUP)]
        for i, h in enumerate(heads):
            r0 = pl.multiple_of(h * 2 * N_KEYS, 2 * N_KEYS)
            q1 = qt_scr[pl.ds(r0, N_KEYS), :].astype(BF16)
            q2 = qt_scr[pl.ds(r0 + N_KEYS, N_KEYS), :].astype(BF16)
            s_scr[2 * i] = _dot(sk_ref[2 * h], q1)
            s_scr[2 * i + 1] = _dot(sk_ref[2 * h + 1], q2)

        ties = _route_group_fast(s_scr, pk_scr, heads, r2_ref, e2_ref, n1_ref, e1z_ref)
        for i, h in enumerate(heads):
            @pl.when(ties[i] > 0.0)
            def _(i=i, h=h):
                rk2, e2, n1, e1z, _ = _route_head(s_scr[2 * i], s_scr[2 * i + 1], True)
                r2_ref[h] = rk2.astype(BF16)
                e2_ref[h] = e2.astype(BF16)
                n1_ref[h] = n1
                e1z_ref[h] = e1z

        return carry

    lax.fori_loop(0, PEER_HEADS // ROUTE_GROUP, group_body, 0)


def _post_call(x, a_out, b_out, mod, mod_row_of_tile, w):
    t = x.shape[0]
    tt = POST_TT
    full = lambda shape: pl.BlockSpec(shape, lambda i: (0,) * len(shape))
    tok = lambda width: pl.BlockSpec((tt, width), lambda i: (i, 0))
    rout = pl.BlockSpec((PEER_HEADS, N_KEYS, tt), lambda i: (0, 0, i))
    return pl.pallas_call(
        _post_kernel,
        grid=(t // tt,),
        in_specs=[tok(D_MODEL), tok(A_WIDTH), tok(A_WIDTH),
                  pl.BlockSpec((1, 6, D_MODEL), lambda i: (mod_row_of_tile(i), 0, 0)),
                  full((D_MODEL, D_MODEL)), full((1, D_MODEL)), full((1, D_MODEL)),
                  full((PEER_HEADS * 2 * N_KEYS, D_MODEL)), full((PEER_HEADS * 2, N_KEYS, N_KEYS))],
        out_specs=[tok(D_MODEL), pl.BlockSpec((D_MODEL, tt), lambda i: (0, i)), rout, rout, rout, rout],
        out_shape=[jax.ShapeDtypeStruct((t, D_MODEL), F32),
                   jax.ShapeDtypeStruct((D_MODEL, t), BF16),
                   jax.ShapeDtypeStruct((PEER_HEADS, N_KEYS, t), BF16),
                   jax.ShapeDtypeStruct((PEER_HEADS, N_KEYS, t), BF16),
                   jax.ShapeDtypeStruct((PEER_HEADS, N_KEYS, t), F32),
                   jax.ShapeDtypeStruct((PEER_HEADS, N_KEYS, t), F32)],
        scratch_shapes=[pltpu.VMEM((PEER_HEADS * 2 * N_KEYS, tt), F32),
                        pltpu.VMEM((2 * ROUTE_GROUP, N_KEYS, tt), F32),
                        pltpu.VMEM((4, PEER_TOPK, SUBLANES, LANES), F32)],
        compiler_params=_cparams(("parallel",)),
        name="post_route",
    )(x, a_out, b_out, mod, w["w_o"], w["ln1_g"], w["ln1_b"], w["w_pqT"], w["sub_keys"])


BF16_ROWS = 16


def _peer_phases(h2t_ref, r2_ref, e2_ref, n1_ref, e1z_ref, u_ref, vt_ref, a_w, a_r, p_w, p_r, acc_scr, valid):
    n_c = h2t_ref.shape[1] // LANES
    n_slab = N_KEYS // BF16_ROWS
    bcast16 = lambda row: jnp.broadcast_to(row, (BF16_ROWS, LANES)).astype(BF16)
    act = _gelu(_dot(u_ref[...], h2t_ref[...])).astype(BF16)
    for c in range(n_c):
        a_w[c] = act[:, c * LANES:(c + 1) * LANES]
    p_all = jnp.concatenate([p_r[c] for c in range(n_c)], axis=1)
    acc_scr[...] += _dot(vt_ref[...], p_all)

    for q0 in range(0, u_ref.shape[0], PEER_QT):
        ils = range(q0 // N_KEYS, (q0 + PEER_QT) // N_KEYS)
        for c in range(n_c):
            l0 = c * LANES
            wgt = {il: [None] * n_slab for il in ils}
            for h in range(PEER_HEADS):
                rows = {il: (bcast16(n1_ref[h, il:il + 1, l0:l0 + LANES] * valid),
                             bcast16(e1z_ref[h, il:il + 1, l0:l0 + LANES])) for il in ils}
                for k in range(n_slab):
                    k0 = k * BF16_ROWS
                    r2s = r2_ref[h, c, k0:k0 + BF16_ROWS, :]
                    e2s = e2_ref[h, c, k0:k0 + BF16_ROWS, :]
                    for il in ils:
                        n16, e16 = rows[il]
                        term = jnp.where(r2s < n16, e2s, jnp.zeros((), BF16)) * e16
                        wgt[il][k] = term if wgt[il][k] is None else wgt[il][k] + term
            for il in ils:
                for k in range(n_slab):
                    k0 = il * N_KEYS + k * BF16_ROWS
                    p_w[c, k0:k0 + BF16_ROWS, :] = a_r[c, k0:k0 + BF16_ROWS, :] * wgt[il][k]


def _peer_kernel(h2t_ref, r2_ref, e2_ref, n1_ref, e1z_ref, u_ref, vt_ref, x1_ref, mod_ref, g2_ref, b2_ref,
                 y_ref, a0, a1, p0, p1, acc_scr, r2_scr, e2_scr, *, n_e, n_blocks):
    s = pl.program_id(0)

    @pl.when(s == 0)
    def _():
        a1[...] = jnp.zeros_like(a1)
        p0[...] = jnp.zeros_like(p0)
        p1[...] = jnp.zeros_like(p1)

    @pl.when(jnp.clip(s - 2, 0, n_blocks - 1) % n_e == 0)
    def _():
        acc_scr[...] = jnp.zeros_like(acc_scr)

    @pl.when(jnp.clip(s - 1, 0, n_blocks - 1) % n_e == 0)
    def _():
        for c in range(r2_scr.shape[1]):
            r2_scr[:, c] = r2_ref[:, :, c * LANES:(c + 1) * LANES]
            e2_scr[:, c] = e2_ref[:, :, c * LANES:(c + 1) * LANES]

    valid = jnp.where((s >= 1) & (s <= n_blocks), 1.0, 0.0).astype(F32)
    args = (h2t_ref, r2_scr, e2_scr, n1_ref, e1z_ref, u_ref, vt_ref)

    @pl.when(s % 2 == 0)
    def _():
        _peer_phases(*args, a0, a1, p1, p0, acc_scr, valid)

    @pl.when(s % 2 == 1)
    def _():
        _peer_phases(*args, a1, a0, p0, p1, acc_scr, valid)

    @pl.when((s >= 2) & ((s - 2) % n_e == n_e - 1))
    def _():
        gate2 = mod_ref[0, 5:6, :]
        peer = acc_scr[...].T
        y_ref[...] = _layer_norm(ALPHA * x1_ref[...] + gate2 * peer, g2_ref[...], b2_ref[...])


def _peer_call(h2t, r2, e2, n1, e1z, x1, mod, mod_row_of_tile, w):
    t = x1.shape[0]
    tt, et = PEER_TT, PEER_ET
    n_i = et // N_KEYS
    n_e = N_EXPERTS // et
    n_blocks = (t // tt) * n_e
    n_c = tt // LANES
    blk = lambda s, lag: jnp.clip(s - lag, 0, n_blocks - 1)
    tile = lambda s, lag: blk(s, lag) // n_e
    exp = lambda s, lag: blk(s, lag) % n_e
    rfull = pl.BlockSpec((PEER_HEADS, N_KEYS, tt), lambda s: (0, 0, tile(s, 1)))
    rrow = pl.BlockSpec((PEER_HEADS, n_i, tt), lambda s: (0, exp(s, 1), tile(s, 1)))
    return pl.pallas_call(
        functools.partial(_peer_kernel, n_e=n_e, n_blocks=n_blocks),
        grid=(n_blocks + 2,),
        in_specs=[pl.BlockSpec((D_MODEL, tt), lambda s: (0, tile(s, 0))),
                  rfull, rfull, rrow, rrow,
                  pl.BlockSpec((et, D_MODEL), lambda s: (exp(s, 0), 0)),
                  pl.BlockSpec((D_MODEL, et), lambda s: (0, exp(s, 2))),
                  pl.BlockSpec((tt, D_MODEL), lambda s: (tile(s, 2), 0)),
                  pl.BlockSpec((1, 6, D_MODEL), lambda s: (mod_row_of_tile(tile(s, 2)), 0, 0)),
                  pl.BlockSpec((1, D_MODEL), lambda s: (0, 0)),
                  pl.BlockSpec((1, D_MODEL), lambda s: (0, 0))],
        out_specs=pl.BlockSpec((tt, D_MODEL), lambda s: (tile(s, 2), 0)),
        out_shape=jax.ShapeDtypeStruct((t, D_MODEL), F32),
        scratch_shapes=[pltpu.VMEM((n_c, et, LANES), BF16), pltpu.VMEM((n_c, et, LANES), BF16),
                        pltpu.VMEM((n_c, et, LANES), BF16), pltpu.VMEM((n_c, et, LANES), BF16),
                        pltpu.VMEM((D_MODEL, tt), F32),
                        pltpu.VMEM((PEER_HEADS, n_c, N_KEYS, LANES), BF16),
                        pltpu.VMEM((PEER_HEADS, n_c, N_KEYS, LANES), BF16)],
        compiler_params=_cparams(("arbitrary",)),
        name="peer_dense",
    )(h2t, r2, e2, n1, e1z, w["u_tab"], w["v_tabT"], x1, mod, w["ln2_g"], w["ln2_b"])


_SWAP64 = np.concatenate([np.arange(16, 32), np.arange(0, 16), np.arange(48, 64), np.arange(32, 48)])


def _rope_tables(n_pos):
    pos = np.arange(n_pos)
    row = (pos // GRID_W).astype(np.float32)
    col = (pos % GRID_W).astype(np.float32)
    f = QK_ROPE // 4
    freqs = (1.0 / (np.float32(ROPE_THETA) ** (np.arange(f, dtype=np.float32) / np.float32(f)))).astype(np.float32)
    ar = row[:, None] * freqs[None, :]
    ac = col[:, None] * freqs[None, :]
    zeros = np.zeros((n_pos, QK_ROPE), np.float32)
    cos_t = np.concatenate([np.cos(ar), np.cos(ar), np.cos(ac), np.cos(ac), zeros], axis=1)
    sin_t = np.concatenate([-np.sin(ar), np.sin(ar), -np.sin(ac), np.sin(ac), zeros], axis=1)
    return jnp.asarray(cos_t, F32), jnp.asarray(sin_t, F32)


def _identity_tables(n_pos):
    ones = np.ones((n_pos, QK_ROPE), np.float32)
    zeros = np.zeros((n_pos, QK_ROPE), np.float32)
    return jnp.asarray(np.concatenate([ones, zeros], axis=1)), jnp.zeros((n_pos, LANES), F32)


def _prep_weights(w_in, w_s, b_s, g_q, w_qb, g_kv, w_kvb, w_o, ln1_g, ln1_b, w_pq, sub_keys, u_tab, v_tab,
                  ln2_g, ln2_b):
    o_kr = 2 * A_WIDTH + Q_RANK + KV_RANK
    w_in_ext = jnp.concatenate([w_in, w_in[:, o_kr + _SWAP64]], axis=1).astype(BF16)
    wq = w_qb.reshape(Q_RANK, MLA_HEADS, QK_NOPE + QK_ROPE)
    w_qb_ext = jnp.concatenate([wq, wq[:, :, QK_NOPE + _SWAP64]], axis=2).reshape(Q_RANK, MLA_HEADS * HEAD_W)
    return {
        "w_in": w_in_ext,
        "w_s": w_s.astype(BF16),
        "bsb": jnp.broadcast_to(b_s[:, :, None], (A_GROUPS, CHUNK, LANES)),
        "g_q": g_q[None, :],
        "w_qb": w_qb_ext.astype(BF16),
        "g_kv": g_kv[None, :],
        "w_kvb": w_kvb.astype(BF16),
        "w_o": w_o.astype(BF16),
        "ln1_g": ln1_g[None, :],
        "ln1_b": ln1_b[None, :],
        "w_pqT": w_pq.T.astype(BF16),
        "sub_keys": sub_keys.reshape(PEER_HEADS * 2, N_KEYS, N_KEYS).astype(BF16),
        "u_tab": u_tab.astype(BF16),
        "v_tabT": v_tab.T.astype(BF16),
        "ln2_g": ln2_g[None, :],
        "ln2_b": ln2_b[None, :],
    }


def _trunk(x, mod, mod_row_of_seq, w, cos_t, sin_t, cache):
    b, n, _ = x.shape
    xf = x.reshape(b * n, D_MODEL)
    row_of = lambda tile_tokens: (lambda i: mod_row_of_seq(i // (n // tile_tokens)))
    a_out, q, k, v, ckv, kr = _pre_call(xf, mod, row_of(PRE_TT), n // PRE_TT, w, cos_t, sin_t)
    q = q.reshape(b, n, -1)
    k = k.reshape(b, n, -1)
    v = v.reshape(b, n, -1)
    if cache is not None:
        k = jnp.concatenate([cache[0], k], axis=1)
        v = jnp.concatenate([cache[1], v], axis=1)
    b_out = _attn_call(q, k, v).reshape(b * n, -1)
    x1, h2t, r2, e2, n1, e1z = _post_call(xf, a_out, b_out, mod, row_of(POST_TT), w)
    y = _peer_call(h2t, r2, e2, n1, e1z, x1, mod, row_of(PEER_TT), w)
    return y.reshape(b, n, D_MODEL), ckv, kr


def kernel(x_prompt, x_sample, cache_ckv, cache_krope, c, c_ctx, w_mod, b_mod, w_in, w_s, b_s, g_q, w_qb, g_kv,
           w_kvb, w_o, ln1_g, ln1_b, w_pq, sub_keys, u_tab, v_tab, ln2_g, ln2_b):
    batch, seq, _ = x_prompt.shape
    dec_batch, dec_seq, _ = x_sample.shape
    past = cache_ckv.shape[2]
    y_p, y_s = x_prompt, x_sample
    ckv_list, kr_list = [], []
    for l in range(DEPTH):
        w = _prep_weights(w_in[l], w_s[l], b_s[l], g_q[l], w_qb[l], g_kv[l], w_kvb[l], w_o[l], ln1_g[l], ln1_b[l],
                          w_pq[l], sub_keys[l], u_tab[l], v_tab[l], ln2_g[l], ln2_b[l])
        c_rows = jnp.concatenate([c_ctx[None, :], c, jnp.zeros((8 - 1 - dec_batch, D_MODEL), F32)], axis=0)
        mod = _mod_call(c_rows, w_mod[l], b_mod[l][None, :]).reshape(8, 6, D_MODEL)

        cos_c, sin_c = _identity_tables(seq)
        y_p, ckv_l, kr_l = _trunk(y_p, mod, lambda s: 0, w, cos_c, sin_c, None)
        ckv_list.append(ckv_l.reshape(batch, seq, KV_RANK))
        kr_list.append(kr_l.reshape(batch, seq, QK_ROPE))

        krp = jnp.pad(cache_krope[:, l].reshape(dec_batch * past, QK_ROPE), ((0, 0), (0, LANES - QK_ROPE)))
        k_c, v_c = _cache_call(cache_ckv[:, l].reshape(dec_batch * past, KV_RANK), krp, w["w_kvb"])
        cache = (k_c.reshape(dec_batch, past, -1), v_c.reshape(dec_batch, past, -1))
        cos_l, sin_l = _rope_tables(dec_seq)
        y_s, _, _ = _trunk(y_s, mod, lambda s: 1 + s, w, cos_l, sin_l, cache)
    new_ckv = jnp.stack(ckv_list, axis=1)
    new_krope = jnp.stack(kr_list, axis=1)
    return (y_p, y_s, new_ckv, new_krope)
```

```python
import functools
import math

import jax
import jax.numpy as jnp
import numpy as np
from jax import lax
from jax.experimental import pallas as pl
from jax.experimental.pallas import tpu as pltpu

F32 = jnp.float32
BF16 = jnp.bfloat16

D_MODEL = 1024
GRID_W = 64
CHUNK = 128
A_GROUPS = 4
A_WIDTH = 512
MLA_HEADS = 4
QK_NOPE = 128
QK_ROPE = 64
V_DIM = 128
Q_RANK = 384
KV_RANK = 256
ROPE_THETA = 10000.0
N_KEYS = 128
PEER_HEADS = 8
PEER_TOPK = 16
N_EXPERTS = N_KEYS * N_KEYS
DEPTH = 1
ALPHA = (2.0 * DEPTH) ** 0.25
EPS = 1e-6

LANES = 128
SUBLANES = 8
HEAD_W = 2 * LANES
IN_EXT = 2 * A_WIDTH + Q_RANK + KV_RANK + 2 * QK_ROPE

PRE_TT = 256
ATT_TQ = 256
POST_TT = 256
ROUTE_GROUP = SUBLANES * LANES // POST_TT
PEER_TT = 256
PEER_ET = 4096
PEER_QT = 256
VMEM_LIMIT = 56 * 1024 * 1024


def _cparams(sem):
    return pltpu.CompilerParams(dimension_semantics=sem, vmem_limit_bytes=VMEM_LIMIT)


def _gelu(x):
    return x * (lax.erf(x * (1.0 / math.sqrt(2.0))) + 1.0) * 0.5


def _dot(a, b):
    return jnp.dot(a, b, preferred_element_type=F32)


def _mod_kernel(c_ref, w_ref, b_ref, o_ref):
    c = c_ref[...]
    sc = c * (1.0 / (1.0 + jnp.exp(-c)))
    o_ref[...] = _dot(sc.astype(BF16), w_ref[...].astype(BF16)) + b_ref[...]


def _mod_call(c_rows, w_mod, b_mod):
    n = w_mod.shape[1]
    bn = 1536
    return pl.pallas_call(
        _mod_kernel,
        grid=(n // bn,),
        in_specs=[pl.BlockSpec((8, D_MODEL), lambda j: (0, 0)),
                  pl.BlockSpec((D_MODEL, bn), lambda j: (0, j)),
                  pl.BlockSpec((1, bn), lambda j: (0, j))],
        out_specs=pl.BlockSpec((8, bn), lambda j: (0, j)),
        out_shape=jax.ShapeDtypeStruct((8, n), F32),
        compiler_params=_cparams(("arbitrary",)),
        name="mod",
    )(c_rows, w_mod, b_mod)


def _rms(x, g):
    return x * lax.rsqrt(jnp.mean(x * x, axis=-1, keepdims=True) + EPS) * g


def _pre_kernel(x_ref, mod_ref, win_ref, ws_ref, bsb_ref, gq_ref, wqb_ref, gkv_ref, wkvb_ref,
                cos_ref, sin_ref, a_ref, q_ref, k_ref, v_ref, ckv_ref, kr_ref):
    x = x_ref[...]
    shift1 = mod_ref[0, 0:1, :]
    scale1 = mod_ref[0, 1:2, :]
    h = x * (1.0 + scale1) + shift1
    z = _dot(h.astype(BF16), win_ref[...])
    tt = x.shape[0]

    for g in range(A_GROUPS):
        lo = g * LANES
        ug = _gelu(z[:, lo:lo + LANES])
        vg = _gelu(z[:, A_WIDTH + lo:A_WIDTH + lo + LANES])
        mu = jnp.mean(vg, axis=-1, keepdims=True)
        vc = vg - mu
        var = jnp.mean(vc * vc, axis=-1, keepdims=True)
        vs = (vc * lax.rsqrt(var + EPS)).astype(BF16)
        for c in range(tt // CHUNK):
            r0 = c * CHUNK
            mixed = _dot(ws_ref[g], vs[r0:r0 + CHUNK, :]) + bsb_ref[g]
            a_ref[r0:r0 + CHUNK, lo:lo + LANES] = (ug[r0:r0 + CHUNK, :] * mixed).astype(BF16)

    cos = cos_ref[...]
    sin = sin_ref[...]
    o_q = 2 * A_WIDTH
    qn = _rms(z[:, o_q:o_q + Q_RANK], gq_ref[...])
    zq = _dot(qn.astype(BF16), wqb_ref[...])
    for hd in range(MLA_HEADS):
        c0 = hd * HEAD_W
        q_ref[:, c0:c0 + LANES] = zq[:, c0:c0 + LANES].astype(BF16)
        blk = zq[:, c0 + LANES:c0 + HEAD_W]
        q_ref[:, c0 + LANES:c0 + HEAD_W] = (blk * cos + pltpu.roll(blk, 64, 1) * sin).astype(BF16)

    o_kv = o_q + Q_RANK
    ckv = _rms(z[:, o_kv:o_kv + KV_RANK], gkv_ref[...])
    ckv_ref[...] = ckv
    o_kr = o_kv + KV_RANK
    kr_ref[...] = z[:, o_kr:o_kr + QK_ROPE]
    kv = _dot(ckv.astype(BF16), wkvb_ref[...])
    kblk = z[:, o_kr:o_kr + LANES]
    yk = (kblk * cos + pltpu.roll(kblk, 64, 1) * sin).astype(BF16)
    for hd in range(MLA_HEADS):
        c0 = hd * HEAD_W
        k_ref[:, c0:c0 + LANES] = kv[:, c0:c0 + LANES].astype(BF16)
        k_ref[:, c0 + LANES:c0 + HEAD_W] = yk
        v_ref[:, hd * V_DIM:(hd + 1) * V_DIM] = kv[:, c0 + LANES:c0 + HEAD_W].astype(BF16)


def _pre_call(x, mod, mod_row_of_tile, pos_tiles, w, cos_t, sin_t):
    t = x.shape[0]
    tt = PRE_TT
    full = lambda shape: pl.BlockSpec(shape, lambda i: (0,) * len(shape))
    tok = lambda width: pl.BlockSpec((tt, width), lambda i: (i, 0))
    return pl.pallas_call(
        _pre_kernel,
        grid=(t // tt,),
        in_specs=[tok(D_MODEL),
                  pl.BlockSpec((1, 6, D_MODEL), lambda i: (mod_row_of_tile(i), 0, 0)),
                  full((D_MODEL, IN_EXT)), full((A_GROUPS, CHUNK, CHUNK)), full((A_GROUPS, CHUNK, LANES)),
                  full((1, Q_RANK)), full((Q_RANK, MLA_HEADS * HEAD_W)),
                  full((1, KV_RANK)), full((KV_RANK, MLA_HEADS * HEAD_W)),
                  pl.BlockSpec((tt, LANES), lambda i: (i % pos_tiles, 0)),
                  pl.BlockSpec((tt, LANES), lambda i: (i % pos_tiles, 0))],
        out_specs=[tok(A_WIDTH), tok(MLA_HEADS * HEAD_W), tok(MLA_HEADS * HEAD_W), tok(MLA_HEADS * V_DIM),
                   tok(KV_RANK), tok(QK_ROPE)],
        out_shape=[jax.ShapeDtypeStruct((t, A_WIDTH), BF16),
                   jax.ShapeDtypeStruct((t, MLA_HEADS * HEAD_W), BF16),
                   jax.ShapeDtypeStruct((t, MLA_HEADS * HEAD_W), BF16),
                   jax.ShapeDtypeStruct((t, MLA_HEADS * V_DIM), BF16),
                   jax.ShapeDtypeStruct((t, KV_RANK), F32),
                   jax.ShapeDtypeStruct((t, QK_ROPE), F32)],
        compiler_params=_cparams(("parallel",)),
        name="pre",
    )(x, mod, w["w_in"], w["w_s"], w["bsb"], w["g_q"], w["w_qb"], w["g_kv"], w["w_kvb"], cos_t, sin_t)


def _cache_kernel(ckv_ref, krp_ref, wkvb_ref, k_ref, v_ref):
    kv = _dot(ckv_ref[...].astype(BF16), wkvb_ref[...])
    yk = krp_ref[...].astype(BF16)
    for hd in range(MLA_HEADS):
        c0 = hd * HEAD_W
        k_ref[:, c0:c0 + LANES] = kv[:, c0:c0 + LANES].astype(BF16)
        k_ref[:, c0 + LANES:c0 + HEAD_W] = yk
        v_ref[:, hd * V_DIM:(hd + 1) * V_DIM] = kv[:, c0 + LANES:c0 + HEAD_W].astype(BF16)


def _cache_call(ckv, krp, w_kvb):
    t = ckv.shape[0]
    tt = 256
    return pl.pallas_call(
        _cache_kernel,
        grid=(t // tt,),
        in_specs=[pl.BlockSpec((tt, KV_RANK), lambda i: (i, 0)),
                  pl.BlockSpec((tt, LANES), lambda i: (i, 0)),
                  pl.BlockSpec((KV_RANK, MLA_HEADS * HEAD_W), lambda i: (0, 0))],
        out_specs=[pl.BlockSpec((tt, MLA_HEADS * HEAD_W), lambda i: (i, 0)),
                   pl.BlockSpec((tt, MLA_HEADS * V_DIM), lambda i: (i, 0))],
        out_shape=[jax.ShapeDtypeStruct((t, MLA_HEADS * HEAD_W), BF16),
                   jax.ShapeDtypeStruct((t, MLA_HEADS * V_DIM), BF16)],
        compiler_params=_cparams(("parallel",)),
        name="cache_kv",
    )(ckv, krp, w_kvb)


def _attn_kernel(q_ref, k_ref, v_ref, o_ref):
    scale = 1.0 / math.sqrt(QK_NOPE + QK_ROPE)
    for hd in range(MLA_HEADS):
        c0 = hd * HEAD_W
        q = q_ref[0, :, c0:c0 + HEAD_W]
        k = k_ref[0, :, c0:c0 + HEAD_W]
        s = lax.dot_general(q, k, (((1,), (1,)), ((), ())), preferred_element_type=F32)
        m = jnp.max(s, axis=-1, keepdims=True)
        p = jnp.exp2((s - m) * (scale * math.log2(math.e)))
        l = jnp.sum(p, axis=-1, keepdims=True)
        o = _dot(p.astype(BF16), v_ref[0, :, hd * V_DIM:(hd + 1) * V_DIM])
        o_ref[0, :, hd * V_DIM:(hd + 1) * V_DIM] = (o / l).astype(BF16)


def _attn_call(q, k, v):
    b, n, _ = q.shape
    m = k.shape[1]
    tq = ATT_TQ
    return pl.pallas_call(
        _attn_kernel,
        grid=(b, n // tq),
        in_specs=[pl.BlockSpec((1, tq, MLA_HEADS * HEAD_W), lambda i, j: (i, j, 0)),
                  pl.BlockSpec((1, m, MLA_HEADS * HEAD_W), lambda i, j: (i, 0, 0)),
                  pl.BlockSpec((1, m, MLA_HEADS * V_DIM), lambda i, j: (i, 0, 0))],
        out_specs=pl.BlockSpec((1, tq, MLA_HEADS * V_DIM), lambda i, j: (i, j, 0)),
        out_shape=jax.ShapeDtypeStruct((b, n, MLA_HEADS * V_DIM), BF16),
        compiler_params=_cparams(("parallel", "arbitrary")),
        name="attn",
    )(q, k, v)


def _layer_norm(x, g, b):
    mu = jnp.mean(x, axis=-1, keepdims=True)
    xc = x - mu
    var = jnp.mean(xc * xc, axis=-1, keepdims=True)
    return xc * lax.rsqrt(var + EPS) * g + b


def _first_max(x, exact):
    m = jnp.max(x, axis=0, keepdims=True)
    hit = x == m
    if exact:
        iota = lax.broadcasted_iota(jnp.int32, x.shape, 0)
        hit = iota == jnp.min(jnp.where(hit, iota, x.shape[0]), axis=0, keepdims=True)
    return m, hit


def _top16(s, exact, want_rank=True, on_value=None):
    nk, tt = s.shape
    iota16 = lax.broadcasted_iota(jnp.int32, (PEER_TOPK, tt), 0)
    rank = jnp.full((nk, tt), float(PEER_TOPK), F32) if want_rank else None
    vals = jnp.zeros((PEER_TOPK, tt), F32) if on_value is None else None
    work = s
    for r in range(PEER_TOPK):
        m, hit = _first_max(work, exact)
        if want_rank:
            rank = jnp.where(hit, float(r), rank)
        work = jnp.where(hit, -jnp.inf, work)
        if on_value is None:
            vals = jnp.where(iota16 == r, m, vals)
        else:
            on_value(r, m)
    return vals, rank


def _young_counts(v1, v2, exact):
    n = jnp.zeros(v1.shape, F32)
    f = v1 + v2[0:1]
    for r in range(PEER_TOPK):
        _, hit = _first_max(f, exact)
        n = jnp.where(hit, n + 1.0, n)
        if r + 1 < PEER_TOPK:
            taken = jnp.sum(jnp.where(hit, n, 0.0), axis=0, keepdims=True)
            nxt = jnp.full(taken.shape, -jnp.inf, F32)
            for b in range(1, PEER_TOPK):
                nxt = jnp.where(taken == float(b), v2[b:b + 1], nxt)
            f = jnp.where(hit, v1 + nxt, f)
    return n


def _pair_weights(v1, v2, exact):
    n = _young_counts(v1, v2, exact)
    e1t = jnp.exp(v1 - v1[0:1])
    e2t = jnp.exp(v2 - v2[0:1])
    za = jnp.zeros_like(e1t)
    for b in range(PEER_TOPK):
        za = za + jnp.where(n > float(b), e2t[b:b + 1], 0.0)
    inv_z = 1.0 / jnp.sum(e1t * za, axis=0, keepdims=True)
    return n, inv_z


def _count(mask):
    return jnp.sum(jnp.where(mask, 1.0, 0.0), axis=0, keepdims=True)


def _route_head(s1, s2, exact):
    v1, rk1 = _top16(s1, exact, want_rank=exact)
    v2, rk2 = _top16(s2, exact)
    n, inv_z = _pair_weights(v1, v2, exact)
    n1 = jnp.zeros_like(s1)
    for a in range(PEER_TOPK):
        is_a = (rk1 == float(a)) if exact else (s1 == v1[a:a + 1, :])
        n1 = jnp.where(is_a, n[a:a + 1, :], n1)
    e2 = jnp.exp(s2 - v2[0:1, :])
    e1z = jnp.exp(s1 - v1[0:1, :]) * inv_z
    k = float(PEER_TOPK)
    picked1 = _count(s1 >= v1[PEER_TOPK - 1:PEER_TOPK, :])
    tie = (picked1 != k) | (_count(rk2 < k) != k) | (jnp.sum(n, axis=0, keepdims=True) != k)
    return rk2, e2, n1, e1z, tie


def _route_group_fast(s_scr, pk_scr, heads, r2_ref, e2_ref, n1_ref, e1z_ref):
    tt = s_scr.shape[2]
    n_ch = tt // LANES
    k = float(PEER_TOPK)
    chunks = lambda row: [row[:, c * LANES:(c + 1) * LANES] for c in range(n_ch)]
    unpack = lambda slot, a, p0: jnp.concatenate(
        [pk_scr[slot, a, p0 + c:p0 + c + 1, :] for c in range(n_ch)], axis=1)
    ties = []
    first = []
    for i, h in enumerate(heads):
        p0 = i * n_ch
        ends = {}

        def pack(slot):
            def on_value(r, m):
                for c, piece in enumerate(chunks(m)):
                    pk_scr[slot, r, p0 + c:p0 + c + 1, :] = piece
                if r in (0, PEER_TOPK - 1):
                    ends[(slot, r)] = m
            return on_value

        s1 = s_scr[2 * i]
        s2 = s_scr[2 * i + 1]
        _top16(s1, False, want_rank=False, on_value=pack(0))
        _, rk2 = _top16(s2, False, on_value=pack(1))
        r2_ref[h] = rk2.astype(BF16)
        e2_ref[h] = jnp.exp(s2 - ends[(1, 0)]).astype(BF16)
        bad = (_count(s1 >= ends[(0, PEER_TOPK - 1)]) != k) | (_count(rk2 < k) != k)
        ties.append(jnp.max(jnp.where(bad, 1.0, 0.0)))
        first.append(ends[(0, 0)])

    n, inv_z = _pair_weights(pk_scr[0], pk_scr[1], False)
    pk_scr[2] = n
    pk_scr[3, 0] = inv_z[0]
    bad_n = jnp.where(jnp.sum(n, axis=0) != k, 1.0, 0.0)
    ties = [jnp.maximum(t, jnp.max(bad_n[i * n_ch:(i + 1) * n_ch, :])) for i, t in enumerate(ties)]

    for i, h in enumerate(heads):
        p0 = i * n_ch
        s1 = s_scr[2 * i]
        n1 = jnp.zeros_like(s1)
        for a in range(PEER_TOPK):
            n1 = jnp.where(s1 == unpack(0, a, p0), unpack(2, a, p0), n1)
        n1_ref[h] = n1
        e1z_ref[h] = jnp.exp(s1 - first[i]) * unpack(3, 0, p0)
    return ties


def _post_kernel(x_ref, a_ref, b_ref, mod_ref, wo_ref, g1_ref, b1_ref, wpq_ref, sk_ref,
                 x1_ref, h2t_ref, r2_ref, e2_ref, n1_ref, e1z_ref, qt_scr, s_scr, pk_scr):
    gate1 = mod_ref[0, 2:3, :]
    shift2 = mod_ref[0, 3:4, :]
    scale2 = mod_ref[0, 4:5, :]
    mix = _dot(a_ref[...], wo_ref[0:A_WIDTH, :]) + _dot(b_ref[...], wo_ref[A_WIDTH:2 * A_WIDTH, :])
    x1 = _layer_norm(ALPHA * x_ref[...] + gate1 * mix, g1_ref[...], b1_ref[...])
    x1_ref[...] = x1
    h2 = x1 * (1.0 + scale2) + shift2
    h2t = h2.T.astype(BF16)
    h2t_ref[...] = h2t
    qt_scr[...] = _dot(wpq_ref[...], h2t)

    def group_body(g, carry):
        heads = [g * ROUTE_GROUP + i for i in range(ROUTE_GROUP)]
        for i, h in enumerate(heads):
            r0 = pl.multiple_of(h * 2 * N_KEYS, 2 * N_KEYS)
            q1 = qt_scr[pl.ds(r0, N_KEYS), :].astype(BF16)
            q2 = qt_scr[pl.ds(r0 + N_KEYS, N_KEYS), :].astype(BF16)
            s_scr[2 * i] = _dot(sk_ref[2 * h], q1)
            s_scr[2 * i + 1] = _dot(sk_ref[2 * h + 1], q2)

        ties = _route_group_fast(s_scr, pk_scr, heads, r2_ref, e2_ref, n1_ref, e1z_ref)
        for i, h in enumerate(heads):
            @pl.when(ties[i] > 0.0)
            def _(i=i, h=h):
                rk2, e2, n1, e1z, _ = _route_head(s_scr[2 * i], s_scr[2 * i + 1], True)
                r2_ref[h] = rk2.astype(BF16)
                e2_ref[h] = e2.astype(BF16)
                n1_ref[h] = n1
                e1z_ref[h] = e1z

        return carry

    lax.fori_loop(0, PEER_HEADS // ROUTE_GROUP, group_body, 0)


def _post_call(x, a_out, b_out, mod, mod_row_of_tile, w):
    t = x.shape[0]
    tt = POST_TT
    full = lambda shape: pl.BlockSpec(shape, lambda i: (0,) * len(shape))
    tok = lambda width: pl.BlockSpec((tt, width), lambda i: (i, 0))
    rout = pl.BlockSpec((PEER_HEADS, N_KEYS, tt), lambda i: (0, 0, i))
    return pl.pallas_call(
        _post_kernel,
        grid=(t // tt,),
        in_specs=[tok(D_MODEL), tok(A_WIDTH), tok(A_WIDTH),
                  pl.BlockSpec((1, 6, D_MODEL), lambda i: (mod_row_of_tile(i), 0, 0)),
                  full((D_MODEL, D_MODEL)), full((1, D_MODEL)), full((1, D_MODEL)),
                  full((PEER_HEADS * 2 * N_KEYS, D_MODEL)), full((PEER_HEADS * 2, N_KEYS, N_KEYS))],
        out_specs=[tok(D_MODEL), pl.BlockSpec((D_MODEL, tt), lambda i: (0, i)), rout, rout, rout, rout],
        out_shape=[jax.ShapeDtypeStruct((t, D_MODEL), F32),
                   jax.ShapeDtypeStruct((D_MODEL, t), BF16),
                   jax.ShapeDtypeStruct((PEER_HEADS, N_KEYS, t), BF16),
                   jax.ShapeDtypeStruct((PEER_HEADS, N_KEYS, t), BF16),
                   jax.ShapeDtypeStruct((PEER_HEADS, N_KEYS, t), F32),
                   jax.ShapeDtypeStruct((PEER_HEADS, N_KEYS, t), F32)],
        scratch_shapes=[pltpu.VMEM((PEER_HEADS * 2 * N_KEYS, tt), F32),
                        pltpu.VMEM((2 * ROUTE_GROUP, N_KEYS, tt), F32),
                        pltpu.VMEM((4, PEER_TOPK, SUBLANES, LANES), F32)],
        compiler_params=_cparams(("parallel",)),
        name="post_route",
    )(x, a_out, b_out, mod, w["w_o"], w["ln1_g"], w["ln1_b"], w["w_pqT"], w["sub_keys"])


BF16_ROWS = 16


def _peer_phases(h2t_ref, r2_ref, e2_ref, n1_ref, e1z_ref, u_ref, vt_ref, a_w, a_r, p_w, p_r, acc_scr, valid):
    n_c = h2t_ref.shape[1] // LANES
    n_slab = N_KEYS // BF16_ROWS
    bcast16 = lambda row: jnp.broadcast_to(row, (BF16_ROWS, LANES)).astype(BF16)
    act = _gelu(_dot(u_ref[...], h2t_ref[...])).astype(BF16)
    for c in range(n_c):
        a_w[c] = act[:, c * LANES:(c + 1) * LANES]
    p_all = jnp.concatenate([p_r[c] for c in range(n_c)], axis=1)
    acc_scr[...] += _dot(vt_ref[...], p_all)

    for q0 in range(0, u_ref.shape[0], PEER_QT):
        ils = range(q0 // N_KEYS, (q0 + PEER_QT) // N_KEYS)
        for c in range(n_c):
            l0 = c * LANES
            wgt = {il: [None] * n_slab for il in ils}
            for h in range(PEER_HEADS):
                rows = {il: (bcast16(n1_ref[h, il:il + 1, l0:l0 + LANES] * valid),
                             bcast16(e1z_ref[h, il:il + 1, l0:l0 + LANES])) for il in ils}
                for k in range(n_slab):
                    k0 = k * BF16_ROWS
                    r2s = r2_ref[h, c, k0:k0 + BF16_ROWS, :]
                    e2s = e2_ref[h, c, k0:k0 + BF16_ROWS, :]
                    for il in ils:
                        n16, e16 = rows[il]
                        term = jnp.where(r2s < n16, e2s, jnp.zeros((), BF16)) * e16
                        wgt[il][k] = term if wgt[il][k] is None else wgt[il][k] + term
            for il in ils:
                for k in range(n_slab):
                    k0 = il * N_KEYS + k * BF16_ROWS
                    p_w[c, k0:k0 + BF16_ROWS, :] = a_r[c, k0:k0 + BF16_ROWS, :] * wgt[il][k]


def _peer_kernel(h2t_ref, r2_ref, e2_ref, n1_ref, e1z_ref, u_ref, vt_ref, x1_ref, mod_ref, g2_ref, b2_ref,
                 y_ref, a0, a1, p0, p1, acc_scr, r2_scr, e2_scr, *, n_e, n_blocks):
    s = pl.program_id(0)

    @pl.when(s == 0)
    def _():
        a1[...] = jnp.zeros_like(a1)
        p0[...] = jnp.zeros_like(p0)
        p1[...] = jnp.zeros_like(p1)

    @pl.when(jnp.clip(s - 2, 0, n_blocks - 1) % n_e == 0)
    def _():
        acc_scr[...] = jnp.zeros_like(acc_scr)

    @pl.when(jnp.clip(s - 1, 0, n_blocks - 1) % n_e == 0)
    def _():
        for c in range(r2_scr.shape[1]):
            r2_scr[:, c] = r2_ref[:, :, c * LANES:(c + 1) * LANES]
            e2_scr[:, c] = e2_ref[:, :, c * LANES:(c + 1) * LANES]

    valid = jnp.where((s >= 1) & (s <= n_blocks), 1.0, 0.0).astype(F32)
    args = (h2t_ref, r2_scr, e2_scr, n1_ref, e1z_ref, u_ref, vt_ref)

    @pl.when(s % 2 == 0)
    def _():
        _peer_phases(*args, a0, a1, p1, p0, acc_scr, valid)

    @pl.when(s % 2 == 1)
    def _():
        _peer_phases(*args, a1, a0, p0, p1, acc_scr, valid)

    @pl.when((s >= 2) & ((s - 2) % n_e == n_e - 1))
    def _():
        gate2 = mod_ref[0, 5:6, :]
        peer = acc_scr[...].T
        y_ref[...] = _layer_norm(ALPHA * x1_ref[...] + gate2 * peer, g2_ref[...], b2_ref[...])


def _peer_call(h2t, r2, e2, n1, e1z, x1, mod, mod_row_of_tile, w):
    t = x1.shape[0]
    tt, et = PEER_TT, PEER_ET
    n_i = et // N_KEYS
    n_e = N_EXPERTS // et
    n_blocks = (t // tt) * n_e
    n_c = tt // LANES
    blk = lambda s, lag: jnp.clip(s - lag, 0, n_blocks - 1)
    tile = lambda s, lag: blk(s, lag) // n_e
    exp = lambda s, lag: blk(s, lag) % n_e
    rfull = pl.BlockSpec((PEER_HEADS, N_KEYS, tt), lambda s: (0, 0, tile(s, 1)))
    rrow = pl.BlockSpec((PEER_HEADS, n_i, tt), lambda s: (0, exp(s, 1), tile(s, 1)))
    return pl.pallas_call(
        functools.partial(_peer_kernel, n_e=n_e, n_blocks=n_blocks),
        grid=(n_blocks + 2,),
        in_specs=[pl.BlockSpec((D_MODEL, tt), lambda s: (0, tile(s, 0))),
                  rfull, rfull, rrow, rrow,
                  pl.BlockSpec((et, D_MODEL), lambda s: (exp(s, 0), 0)),
                  pl.BlockSpec((D_MODEL, et), lambda s: (0, exp(s, 2))),
                  pl.BlockSpec((tt, D_MODEL), lambda s: (tile(s, 2), 0)),
                  pl.BlockSpec((1, 6, D_MODEL), lambda s: (mod_row_of_tile(tile(s, 2)), 0, 0)),
                  pl.BlockSpec((1, D_MODEL), lambda s: (0, 0)),
                  pl.BlockSpec((1, D_MODEL), lambda s: (0, 0))],
        out_specs=pl.BlockSpec((tt, D_MODEL), lambda s: (tile(s, 2), 0)),
        out_shape=jax.ShapeDtypeStruct((t, D_MODEL), F32),
        scratch_shapes=[pltpu.VMEM((n_c, et, LANES), BF16), pltpu.VMEM((n_c, et, LANES), BF16),
                        pltpu.VMEM((n_c, et, LANES), BF16), pltpu.VMEM((n_c, et, LANES), BF16),
                        pltpu.VMEM((D_MODEL, tt), F32),
                        pltpu.VMEM((PEER_HEADS, n_c, N_KEYS, LANES), BF16),
                        pltpu.VMEM((PEER_HEADS, n_c, N_KEYS, LANES), BF16)],
        compiler_params=_cparams(("arbitrary",)),
        name="peer_dense",
    )(h2t, r2, e2, n1, e1z, w["u_tab"], w["v_tabT"], x1, mod, w["ln2_g"], w["ln2_b"])


_SWAP64 = np.concatenate([np.arange(16, 32), np.arange(0, 16), np.arange(48, 64), np.arange(32, 48)])


def _rope_tables(n_pos):
    pos = np.arange(n_pos)
    row = (pos // GRID_W).astype(np.float32)
    col = (pos % GRID_W).astype(np.float32)
    f = QK_ROPE // 4
    freqs = (1.0 / (np.float32(ROPE_THETA) ** (np.arange(f, dtype=np.float32) / np.float32(f)))).astype(np.float32)
    ar = row[:, None] * freqs[None, :]
    ac = col[:, None] * freqs[None, :]
    zeros = np.zeros((n_pos, QK_ROPE), np.float32)
    cos_t = np.concatenate([np.cos(ar), np.cos(ar), np.cos(ac), np.cos(ac), zeros], axis=1)
    sin_t = np.concatenate([-np.sin(ar), np.sin(ar), -np.sin(ac), np.sin(ac), zeros], axis=1)
    return jnp.asarray(cos_t, F32), jnp.asarray(sin_t, F32)


def _identity_tables(n_pos):
    ones = np.ones((n_pos, QK_ROPE), np.float32)
    zeros = np.zeros((n_pos, QK_ROPE), np.float32)
    return jnp.asarray(np.concatenate([ones, zeros], axis=1)), jnp.zeros((n_pos, LANES), F32)


def _prep_weights(w_in, w_s, b_s, g_q, w_qb, g_kv, w_kvb, w_o, ln1_g, ln1_b, w_pq, sub_keys, u_tab, v_tab,
                  ln2_g, ln2_b):
    o_kr = 2 * A_WIDTH + Q_RANK + KV_RANK
    w_in_ext = jnp.concatenate([w_in, w_in[:, o_kr + _SWAP64]], axis=1).astype(BF16)
    wq = w_qb.reshape(Q_RANK, MLA_HEADS, QK_NOPE + QK_ROPE)
    w_qb_ext = jnp.concatenate([wq, wq[:, :, QK_NOPE + _SWAP64]], axis=2).reshape(Q_RANK, MLA_HEADS * HEAD_W)
    return {
        "w_in": w_in_ext,
        "w_s": w_s.astype(BF16),
        "bsb": jnp.broadcast_to(b_s[:, :, None], (A_GROUPS, CHUNK, LANES)),
        "g_q": g_q[None, :],
        "w_qb": w_qb_ext.astype(BF16),
        "g_kv": g_kv[None, :],
        "w_kvb": w_kvb.astype(BF16),
        "w_o": w_o.astype(BF16),
        "ln1_g": ln1_g[None, :],
        "ln1_b": ln1_b[None, :],
        "w_pqT": w_pq.T.astype(BF16),
        "sub_keys": sub_keys.reshape(PEER_HEADS * 2, N_KEYS, N_KEYS).astype(BF16),
        "u_tab": u_tab.astype(BF16),
        "v_tabT": v_tab.T.astype(BF16),
        "ln2_g": ln2_g[None, :],
        "ln2_b": ln2_b[None, :],
    }


def _trunk(x, mod, mod_row_of_seq, w, cos_t, sin_t, cache):
    b, n, _ = x.shape
    xf = x.reshape(b * n, D_MODEL)
    row_of = lambda tile_tokens: (lambda i: mod_row_of_seq(i // (n // tile_tokens)))
    a_out, q, k, v, ckv, kr = _pre_call(xf, mod, row_of(PRE_TT), n // PRE_TT, w, cos_t, sin_t)
    q = q.reshape(b, n, -1)
    k = k.reshape(b, n, -1)
    v = v.reshape(b, n, -1)
    if cache is not None:
        k = jnp.concatenate([cache[0], k], axis=1)
        v = jnp.concatenate([cache[1], v], axis=1)
    b_out = _attn_call(q, k, v).reshape(b * n, -1)
    x1, h2t, r2, e2, n1, e1z = _post_call(xf, a_out, b_out, mod, row_of(POST_TT), w)
    y = _peer_call(h2t, r2, e2, n1, e1z, x1, mod, row_of(PEER_TT), w)
    return y.reshape(b, n, D_MODEL), ckv, kr


def kernel(x_prompt, x_sample, cache_ckv, cache_krope, c, c_ctx, w_mod, b_mod, w_in, w_s, b_s, g_q, w_qb, g_kv,
           w_kvb, w_o, ln1_g, ln1_b, w_pq, sub_keys, u_tab, v_tab, ln2_g, ln2_b):
    batch, seq, _ = x_prompt.shape
    dec_batch, dec_seq, _ = x_sample.shape
    past = cache_ckv.shape[2]
    y_p, y_s = x_prompt, x_sample
    ckv_list, kr_list = [], []
    for l in range(DEPTH):
        w = _prep_weights(w_in[l], w_s[l], b_s[l], g_q[l], w_qb[l], g_kv[l], w_kvb[l], w_o[l], ln1_g[l], ln1_b[l],
                          w_pq[l], sub_keys[l], u_tab[l], v_tab[l], ln2_g[l], ln2_b[l])
        c_rows = jnp.concatenate([c_ctx[None, :], c, jnp.zeros((8 - 1 - dec_batch, D_MODEL), F32)], axis=0)
        mod = _mod_call(c_rows, w_mod[l], b_mod[l][None, :]).reshape(8, 6, D_MODEL)

        cos_c, sin_c = _identity_tables(seq)
        y_p, ckv_l, kr_l = _trunk(y_p, mod, lambda s: 0, w, cos_c, sin_c, None)
        ckv_list.append(ckv_l.reshape(batch, seq, KV_RANK))
        kr_list.append(kr_l.reshape(batch, seq, QK_ROPE))

        krp = jnp.pad(cache_krope[:, l].reshape(dec_batch * past, QK_ROPE), ((0, 0), (0, LANES - QK_ROPE)))
        k_c, v_c = _cache_call(cache_ckv[:, l].reshape(dec_batch * past, KV_RANK), krp, w["w_kvb"])
        cache = (k_c.reshape(dec_batch, past, -1), v_c.reshape(dec_batch, past, -1))
        cos_l, sin_l = _rope_tables(dec_seq)
        y_s, _, _ = _trunk(y_s, mod, lambda s: 1 + s, w, cos_l, sin_l, cache)
    new_ckv = jnp.stack(ckv_list, axis=1)
    new_krope = jnp.stack(kr_list, axis=1)
    return (y_p, y_s, new_ckv, new_krope)
```

```python
import functools
import math

import jax
import jax.numpy as jnp
import numpy as np
from jax import lax
from jax.experimental import pallas as pl
from jax.experimental.pallas import tpu as pltpu

F32 = jnp.float32
BF16 = jnp.bfloat16

D_MODEL = 1024
GRID_W = 64
CHUNK = 128
A_GROUPS = 4
A_WIDTH = 512
MLA_HEADS = 4
QK_NOPE = 128
QK_ROPE = 64
V_DIM = 128
Q_RANK = 384
KV_RANK = 256
ROPE_THETA = 10000.0
N_KEYS = 128
PEER_HEADS = 8
PEER_TOPK = 16
N_EXPERTS = N_KEYS * N_KEYS
DEPTH = 1
ALPHA = (2.0 * DEPTH) ** 0.25
EPS = 1e-6

LANES = 128
SUBLANES = 8
HEAD_W = 2 * LANES
IN_EXT = 2 * A_WIDTH + Q_RANK + KV_RANK + 2 * QK_ROPE

PRE_TT = 256
ATT_TQ = 256
POST_TT = 256
ROUTE_GROUP = SUBLANES * LANES // POST_TT
PEER_TT = 256
PEER_ET = 4096
PEER_QT = 256
PEER_MT = 1024
VMEM_LIMIT = 56 * 1024 * 1024


def _cparams(sem):
    return pltpu.CompilerParams(dimension_semantics=sem, vmem_limit_bytes=VMEM_LIMIT)


def _gelu(x):
    return x * (lax.erf(x * (1.0 / math.sqrt(2.0))) + 1.0) * 0.5


def _dot(a, b):
    return jnp.dot(a, b, preferred_element_type=F32)


def _mod_kernel(c_ref, w_ref, b_ref, o_ref):
    c = c_ref[...]
    sc = c * (1.0 / (1.0 + jnp.exp(-c)))
    o_ref[...] = _dot(sc.astype(BF16), w_ref[...].astype(BF16)) + b_ref[...]


def _mod_call(c_rows, w_mod, b_mod):
    n = w_mod.shape[1]
    bn = 1536
    return pl.pallas_call(
        _mod_kernel,
        grid=(n // bn,),
        in_specs=[pl.BlockSpec((8, D_MODEL), lambda j: (0, 0)),
                  pl.BlockSpec((D_MODEL, bn), lambda j: (0, j)),
                  pl.BlockSpec((1, bn), lambda j: (0, j))],
        out_specs=pl.BlockSpec((8, bn), lambda j: (0, j)),
        out_shape=jax.ShapeDtypeStruct((8, n), F32),
        compiler_params=_cparams(("arbitrary",)),
        name="mod",
    )(c_rows, w_mod, b_mod)


def _rms(x, g):
    return x * lax.rsqrt(jnp.mean(x * x, axis=-1, keepdims=True) + EPS) * g


def _pre_kernel(x_ref, mod_ref, win_ref, ws_ref, bsb_ref, gq_ref, wqb_ref, gkv_ref, wkvb_ref,
                cos_ref, sin_ref, a_ref, q_ref, k_ref, v_ref, ckv_ref, kr_ref):
    x = x_ref[...]
    shift1 = mod_ref[0, 0:1, :]
    scale1 = mod_ref[0, 1:2, :]
    h = x * (1.0 + scale1) + shift1
    hb = h.astype(BF16)
    tt = x.shape[0]
    o_q = 2 * A_WIDTH
    o_kv = o_q + Q_RANK
    z_u = _dot(hb, win_ref[:, 0:A_WIDTH])
    z_v = _dot(hb, win_ref[:, A_WIDTH:o_q])

    for g in range(A_GROUPS):
        lo = g * LANES
        ug = _gelu(z_u[:, lo:lo + LANES])
        vg = _gelu(z_v[:, lo:lo + LANES])
        mu = jnp.mean(vg, axis=-1, keepdims=True)
        vc = vg - mu
        var = jnp.mean(vc * vc, axis=-1, keepdims=True)
        vs = (vc * lax.rsqrt(var + EPS)).astype(BF16)
        for c in range(tt // CHUNK):
            r0 = c * CHUNK
            mixed = _dot(ws_ref[g], vs[r0:r0 + CHUNK, :]) + bsb_ref[g]
            a_ref[r0:r0 + CHUNK, lo:lo + LANES] = (ug[r0:r0 + CHUNK, :] * mixed).astype(BF16)

    cos = cos_ref[...]
    sin = sin_ref[...]
    qn = _rms(_dot(hb, win_ref[:, o_q:o_kv]), gq_ref[...])
    zq = _dot(qn.astype(BF16), wqb_ref[...])
    for hd in range(MLA_HEADS):
        c0 = hd * HEAD_W
        q_ref[:, c0:c0 + LANES] = zq[:, c0:c0 + LANES].astype(BF16)
        blk = zq[:, c0 + LANES:c0 + HEAD_W]
        q_ref[:, c0 + LANES:c0 + HEAD_W] = (blk * cos + pltpu.roll(blk, 64, 1) * sin).astype(BF16)

    z_k = _dot(hb, win_ref[:, o_kv:IN_EXT])
    ckv = _rms(z_k[:, 0:KV_RANK], gkv_ref[...])
    ckv_ref[...] = ckv
    kr_ref[...] = z_k[:, KV_RANK:KV_RANK + QK_ROPE]
    kv = _dot(ckv.astype(BF16), wkvb_ref[...])
    kblk = z_k[:, KV_RANK:KV_RANK + LANES]
    yk = (kblk * cos + pltpu.roll(kblk, 64, 1) * sin).astype(BF16)
    for hd in range(MLA_HEADS):
        c0 = hd * HEAD_W
        k_ref[:, c0:c0 + LANES] = kv[:, c0:c0 + LANES].astype(BF16)
        k_ref[:, c0 + LANES:c0 + HEAD_W] = yk
        v_ref[:, hd * V_DIM:(hd + 1) * V_DIM] = kv[:, c0 + LANES:c0 + HEAD_W].astype(BF16)


def _pre_call(x, mod, mod_row_of_tile, pos_tiles, w, cos_t, sin_t):
    t = x.shape[0]
    tt = PRE_TT
    full = lambda shape: pl.BlockSpec(shape, lambda i: (0,) * len(shape))
    tok = lambda width: pl.BlockSpec((tt, width), lambda i: (i, 0))
    return pl.pallas_call(
        _pre_kernel,
        grid=(t // tt,),
        in_specs=[tok(D_MODEL),
                  pl.BlockSpec((1, 6, D_MODEL), lambda i: (mod_row_of_tile(i), 0, 0)),
                  full((D_MODEL, IN_EXT)), full((A_GROUPS, CHUNK, CHUNK)), full((A_GROUPS, CHUNK, LANES)),
                  full((1, Q_RANK)), full((Q_RANK, MLA_HEADS * HEAD_W)),
                  full((1, KV_RANK)), full((KV_RANK, MLA_HEADS * HEAD_W)),
                  pl.BlockSpec((tt, LANES), lambda i: (i % pos_tiles, 0)),
                  pl.BlockSpec((tt, LANES), lambda i: (i % pos_tiles, 0))],
        out_specs=[tok(A_WIDTH), tok(MLA_HEADS * HEAD_W), tok(MLA_HEADS * HEAD_W), tok(MLA_HEADS * V_DIM),
                   tok(KV_RANK), tok(QK_ROPE)],
        out_shape=[jax.ShapeDtypeStruct((t, A_WIDTH), BF16),
                   jax.ShapeDtypeStruct((t, MLA_HEADS * HEAD_W), BF16),
                   jax.ShapeDtypeStruct((t, MLA_HEADS * HEAD_W), BF16),
                   jax.ShapeDtypeStruct((t, MLA_HEADS * V_DIM), BF16),
                   jax.ShapeDtypeStruct((t, KV_RANK), F32),
                   jax.ShapeDtypeStruct((t, QK_ROPE), F32)],
        compiler_params=_cparams(("parallel",)),
        name="pre",
    )(x, mod, w["w_in"], w["w_s"], w["bsb"], w["g_q"], w["w_qb"], w["g_kv"], w["w_kvb"], cos_t, sin_t)


def _cache_kernel(ckv_ref, krp_ref, wkvb_ref, k_ref, v_ref):
    kv = _dot(ckv_ref[...].astype(BF16), wkvb_ref[...])
    yk = krp_ref[...].astype(BF16)
    for hd in range(MLA_HEADS):
        c0 = hd * HEAD_W
        k_ref[:, c0:c0 + LANES] = kv[:, c0:c0 + LANES].astype(BF16)
        k_ref[:, c0 + LANES:c0 + HEAD_W] = yk
        v_ref[:, hd * V_DIM:(hd + 1) * V_DIM] = kv[:, c0 + LANES:c0 + HEAD_W].astype(BF16)


def _cache_call(ckv, krp, w_kvb):
    t = ckv.shape[0]
    tt = 256
    return pl.pallas_call(
        _cache_kernel,
        grid=(t // tt,),
        in_specs=[pl.BlockSpec((tt, KV_RANK), lambda i: (i, 0)),
                  pl.BlockSpec((tt, LANES), lambda i: (i, 0)),
                  pl.BlockSpec((KV_RANK, MLA_HEADS * HEAD_W), lambda i: (0, 0))],
        out_specs=[pl.BlockSpec((tt, MLA_HEADS * HEAD_W), lambda i: (i, 0)),
                   pl.BlockSpec((tt, MLA_HEADS * V_DIM), lambda i: (i, 0))],
        out_shape=[jax.ShapeDtypeStruct((t, MLA_HEADS * HEAD_W), BF16),
                   jax.ShapeDtypeStruct((t, MLA_HEADS * V_DIM), BF16)],
        compiler_params=_cparams(("parallel",)),
        name="cache_kv",
    )(ckv, krp, w_kvb)


def _attn_kernel(q_ref, k_ref, v_ref, o_ref):
    scale = 1.0 / math.sqrt(QK_NOPE + QK_ROPE)
    for hd in range(MLA_HEADS):
        c0 = hd * HEAD_W
        q = q_ref[0, :, c0:c0 + HEAD_W]
        k = k_ref[0, :, c0:c0 + HEAD_W]
        s = lax.dot_general(q, k, (((1,), (1,)), ((), ())), preferred_element_type=F32)
        m = jnp.max(s, axis=-1, keepdims=True)
        p = jnp.exp2((s - m) * (scale * math.log2(math.e)))
        l = jnp.sum(p, axis=-1, keepdims=True)
        o = _dot(p.astype(BF16), v_ref[0, :, hd * V_DIM:(hd + 1) * V_DIM])
        o_ref[0, :, hd * V_DIM:(hd + 1) * V_DIM] = (o / l).astype(BF16)


def _attn_call(q, k, v):
    b, n, _ = q.shape
    m = k.shape[1]
    tq = ATT_TQ
    return pl.pallas_call(
        _attn_kernel,
        grid=(b, n // tq),
        in_specs=[pl.BlockSpec((1, tq, MLA_HEADS * HEAD_W), lambda i, j: (i, j, 0)),
                  pl.BlockSpec((1, m, MLA_HEADS * HEAD_W), lambda i, j: (i, 0, 0)),
                  pl.BlockSpec((1, m, MLA_HEADS * V_DIM), lambda i, j: (i, 0, 0))],
        out_specs=pl.BlockSpec((1, tq, MLA_HEADS * V_DIM), lambda i, j: (i, j, 0)),
        out_shape=jax.ShapeDtypeStruct((b, n, MLA_HEADS * V_DIM), BF16),
        compiler_params=_cparams(("parallel", "arbitrary")),
        name="attn",
    )(q, k, v)


def _layer_norm(x, g, b):
    mu = jnp.mean(x, axis=-1, keepdims=True)
    xc = x - mu
    var = jnp.mean(xc * xc, axis=-1, keepdims=True)
    return xc * lax.rsqrt(var + EPS) * g + b


def _first_max(x, exact):
    m = jnp.max(x, axis=0, keepdims=True)
    hit = x == m
    if exact:
        iota = lax.broadcasted_iota(jnp.int32, x.shape, 0)
        hit = iota == jnp.min(jnp.where(hit, iota, x.shape[0]), axis=0, keepdims=True)
    return m, hit


def _top16(s, exact, want_rank=True, on_value=None):
    nk, tt = s.shape
    iota16 = lax.broadcasted_iota(jnp.int32, (PEER_TOPK, tt), 0)
    rank = jnp.full((nk, tt), float(PEER_TOPK), F32) if want_rank else None
    vals = jnp.zeros((PEER_TOPK, tt), F32) if on_value is None else None
    work = s
    for r in range(PEER_TOPK):
        m, hit = _first_max(work, exact)
        if want_rank:
            rank = jnp.where(hit, float(r), rank)
        work = jnp.where(hit, -jnp.inf, work)
        if on_value is None:
            vals = jnp.where(iota16 == r, m, vals)
        else:
            on_value(r, m)
    return vals, rank


def _young_counts(v1, v2, exact):
    n = jnp.zeros(v1.shape, F32)
    f = v1 + v2[0:1]
    for r in range(PEER_TOPK):
        _, hit = _first_max(f, exact)
        n = jnp.where(hit, n + 1.0, n)
        if r + 1 < PEER_TOPK:
            taken = jnp.sum(jnp.where(hit, n, 0.0), axis=0, keepdims=True)
            nxt = jnp.full(taken.shape, -jnp.inf, F32)
            for b in range(1, PEER_TOPK):
                nxt = jnp.where(taken == float(b), v2[b:b + 1], nxt)
            f = jnp.where(hit, v1 + nxt, f)
    return n


def _pair_weights(v1, v2, exact):
    n = _young_counts(v1, v2, exact)
    e1t = jnp.exp(v1 - v1[0:1])
    e2t = jnp.exp(v2 - v2[0:1])
    za = jnp.zeros_like(e1t)
    for b in range(PEER_TOPK):
        za = za + jnp.where(n > float(b), e2t[b:b + 1], 0.0)
    inv_z = 1.0 / jnp.sum(e1t * za, axis=0, keepdims=True)
    return n, inv_z


def _count(mask):
    return jnp.sum(jnp.where(mask, 1.0, 0.0), axis=0, keepdims=True)


def _route_head(s1, s2, exact):
    v1, rk1 = _top16(s1, exact, want_rank=exact)
    v2, rk2 = _top16(s2, exact)
    n, inv_z = _pair_weights(v1, v2, exact)
    n1 = jnp.zeros_like(s1)
    for a in range(PEER_TOPK):
        is_a = (rk1 == float(a)) if exact else (s1 == v1[a:a + 1, :])
        n1 = jnp.where(is_a, n[a:a + 1, :], n1)
    e2 = jnp.exp(s2 - v2[0:1, :])
    e1z = jnp.exp(s1 - v1[0:1, :]) * inv_z
    k = float(PEER_TOPK)
    picked1 = _count(s1 >= v1[PEER_TOPK - 1:PEER_TOPK, :])
    tie = (picked1 != k) | (_count(rk2 < k) != k) | (jnp.sum(n, axis=0, keepdims=True) != k)
    return rk2, e2, n1, e1z, tie


def _route_group_fast(s_scr, pk_scr, heads, r2_ref, e2_ref, n1_ref, e1z_ref):
    tt = s_scr.shape[2]
    n_ch = tt // LANES
    k = float(PEER_TOPK)
    chunks = lambda row: [row[:, c * LANES:(c + 1) * LANES] for c in range(n_ch)]
    unpack = lambda slot, a, p0: jnp.concatenate(
        [pk_scr[slot, a, p0 + c:p0 + c + 1, :] for c in range(n_ch)], axis=1)
    ties = []
    first = []
    for i, h in enumerate(heads):
        p0 = i * n_ch
        ends = {}

        def pack(slot):
            def on_value(r, m):
                for c, piece in enumerate(chunks(m)):
                    pk_scr[slot, r, p0 + c:p0 + c + 1, :] = piece
                if r in (0, PEER_TOPK - 1):
                    ends[(slot, r)] = m
            return on_value

        s1 = s_scr[2 * i]
        s2 = s_scr[2 * i + 1]
        _top16(s1, False, want_rank=False, on_value=pack(0))
        _, rk2 = _top16(s2, False, on_value=pack(1))
        r2_ref[h] = rk2.astype(BF16)
        e2_ref[h] = jnp.exp(s2 - ends[(1, 0)]).astype(BF16)
        bad = (_count(s1 >= ends[(0, PEER_TOPK - 1)]) != k) | (_count(rk2 < k) != k)
        ties.append(jnp.max(jnp.where(bad, 1.0, 0.0)))
        first.append(ends[(0, 0)])

    n, inv_z = _pair_weights(pk_scr[0], pk_scr[1], False)
    pk_scr[2] = n
    pk_scr[3, 0] = inv_z[0]
    bad_n = jnp.where(jnp.sum(n, axis=0) != k, 1.0, 0.0)
    ties = [jnp.maximum(t, jnp.max(bad_n[i * n_ch:(i + 1) * n_ch, :])) for i, t in enumerate(ties)]

    for i, h in enumerate(heads):
        p0 = i * n_ch
        s1 = s_scr[2 * i]
        n1 = jnp.zeros_like(s1)
        for a in range(PEER_TOPK):
            n1 = jnp.where(s1 == unpack(0, a, p0), unpack(2, a, p0), n1)
        n1_ref[h] = n1
        e1z_ref[h] = jnp.exp(s1 - first[i]) * unpack(3, 0, p0)
    return ties


def _post_kernel(x_ref, a_ref, b_ref, mod_ref, wo_ref, g1_ref, b1_ref, wpq_ref, sk_ref,
                 x1_ref, h2t_ref, r2_ref, e2_ref, n1_ref, e1z_ref, qt_scr, s_scr, pk_scr):
    gate1 = mod_ref[0, 2:3, :]
    shift2 = mod_ref[0, 3:4, :]
    scale2 = mod_ref[0, 4:5, :]
    mix = _dot(a_ref[...], wo_ref[0:A_WIDTH, :]) + _dot(b_ref[...], wo_ref[A_WIDTH:2 * A_WIDTH, :])
    x1 = _layer_norm(ALPHA * x_ref[...] + gate1 * mix, g1_ref[...], b1_ref[...])
    x1_ref[...] = x1
    h2 = x1 * (1.0 + scale2) + shift2
    h2t = h2.T.astype(BF16)
    h2t_ref[...] = h2t
    for m0 in range(0, qt_scr.shape[0], 4 * N_KEYS):
        qt_scr[m0:m0 + 4 * N_KEYS, :] = _dot(wpq_ref[m0:m0 + 4 * N_KEYS, :], h2t)

    def group_body(g, carry):
        heads = [g * ROUTE_GROUP + i for i in range(ROUTE_GROUP)]
        for i, h in enumerate(heads):
            r0 = pl.multiple_of(h * 2 * N_KEYS, 2 * N_KEYS)
            q1 = qt_scr[pl.ds(r0, N_KEYS), :].astype(BF16)
            q2 = qt_scr[pl.ds(r0 + N_KEYS, N_KEYS), :].astype(BF16)
            s_scr[2 * i] = _dot(sk_ref[2 * h], q1)
            s_scr[2 * i + 1] = _dot(sk_ref[2 * h + 1], q2)

        ties = _route_group_fast(s_scr, pk_scr, heads, r2_ref, e2_ref, n1_ref, e1z_ref)
        for i, h in enumerate(heads):
            @pl.when(ties[i] > 0.0)
            def _(i=i, h=h):
                rk2, e2, n1, e1z, _ = _route_head(s_scr[2 * i], s_scr[2 * i + 1], True)
                r2_ref[h] = rk2.astype(BF16)
                e2_ref[h] = e2.astype(BF16)
                n1_ref[h] = n1
                e1z_ref[h] = e1z

        return carry

    lax.fori_loop(0, PEER_HEADS // ROUTE_GROUP, group_body, 0)


def _post_call(x, a_out, b_out, mod, mod_row_of_tile, w):
    t = x.shape[0]
    tt = POST_TT
    full = lambda shape: pl.BlockSpec(shape, lambda i: (0,) * len(shape))
    tok = lambda width: pl.BlockSpec((tt, width), lambda i: (i, 0))
    rout = pl.BlockSpec((PEER_HEADS, N_KEYS, tt), lambda i: (0, 0, i))
    return pl.pallas_call(
        _post_kernel,
        grid=(t // tt,),
        in_specs=[tok(D_MODEL), tok(A_WIDTH), tok(A_WIDTH),
                  pl.BlockSpec((1, 6, D_MODEL), lambda i: (mod_row_of_tile(i), 0, 0)),
                  full((D_MODEL, D_MODEL)), full((1, D_MODEL)), full((1, D_MODEL)),
                  full((PEER_HEADS * 2 * N_KEYS, D_MODEL)), full((PEER_HEADS * 2, N_KEYS, N_KEYS))],
        out_specs=[tok(D_MODEL), pl.BlockSpec((D_MODEL, tt), lambda i: (0, i)), rout, rout, rout, rout],
        out_shape=[jax.ShapeDtypeStruct((t, D_MODEL), F32),
                   jax.ShapeDtypeStruct((D_MODEL, t), BF16),
                   jax.ShapeDtypeStruct((PEER_HEADS, N_KEYS, t), BF16),
                   jax.ShapeDtypeStruct((PEER_HEADS, N_KEYS, t), BF16),
                   jax.ShapeDtypeStruct((PEER_HEADS, N_KEYS, t), F32),
                   jax.ShapeDtypeStruct((PEER_HEADS, N_KEYS, t), F32)],
        scratch_shapes=[pltpu.VMEM((PEER_HEADS * 2 * N_KEYS, tt), F32),
                        pltpu.VMEM((2 * ROUTE_GROUP, N_KEYS, tt), F32),
                        pltpu.VMEM((4, PEER_TOPK, SUBLANES, LANES), F32)],
        compiler_params=_cparams(("parallel",)),
        name="post_route",
    )(x, a_out, b_out, mod, w["w_o"], w["ln1_g"], w["ln1_b"], w["w_pqT"], w["sub_keys"])


BF16_ROWS = 16


def _peer_matmuls(h2t_ref, u_ref, vt_ref, a_w, p_r, acc_scr, first, second):
    n_c = h2t_ref.shape[1] // LANES
    if first:
        for m0 in range(0, u_ref.shape[0], PEER_MT):
            act = _gelu(_dot(u_ref[m0:m0 + PEER_MT, :], h2t_ref[...])).astype(BF16)
            for c in range(n_c):
                a_w[c, m0:m0 + PEER_MT, :] = act[:, c * LANES:(c + 1) * LANES]
    if second:
        p_all = jnp.concatenate([p_r[c] for c in range(n_c)], axis=1)
        mt = min(PEER_MT, D_MODEL)
        for m0 in range(0, D_MODEL, mt):
            acc_scr[m0:m0 + mt, :] += _dot(vt_ref[m0:m0 + mt, :], p_all)


def _peer_weighting(r2_ref, e2_ref, n1_ref, e1z_ref, a_r, p_w):
    n_c, et, _ = a_r.shape
    n_slab = N_KEYS // BF16_ROWS
    bcast16 = lambda row: jnp.broadcast_to(row, (BF16_ROWS, LANES)).astype(BF16)
    for q0 in range(0, et, PEER_QT):
        ils = range(q0 // N_KEYS, (q0 + PEER_QT) // N_KEYS)
        for c in range(n_c):
            l0 = c * LANES
            wgt = {il: [None] * n_slab for il in ils}
            for h in range(PEER_HEADS):
                rows = {il: (bcast16(n1_ref[h, il:il + 1, l0:l0 + LANES]),
                             bcast16(e1z_ref[h, il:il + 1, l0:l0 + LANES])) for il in ils}
                for k in range(n_slab):
                    k0 = k * BF16_ROWS
                    r2s = r2_ref[h, c, k0:k0 + BF16_ROWS, :]
                    e2s = e2_ref[h, c, k0:k0 + BF16_ROWS, :]
                    for il in ils:
                        n16, e16 = rows[il]
                        term = jnp.where(r2s < n16, e2s, jnp.zeros((), BF16)) * e16
                        wgt[il][k] = term if wgt[il][k] is None else wgt[il][k] + term
            for il in ils:
                for k in range(n_slab):
                    k0 = il * N_KEYS + k * BF16_ROWS
                    p_w[c, k0:k0 + BF16_ROWS, :] = a_r[c, k0:k0 + BF16_ROWS, :] * wgt[il][k]


def _peer_kernel(h2t_ref, r2_ref, e2_ref, n1_ref, e1z_ref, u_ref, vt_ref, x1_ref, mod_ref, g2_ref, b2_ref,
                 y_ref, a_scr, p_scr, acc_scr, r2_scr, e2_scr, *, n_e, n_blocks):
    s = pl.program_id(0)
    slot = s % 2
    a_w, a_r = a_scr.at[slot], a_scr.at[1 - slot]
    p_w, p_r = p_scr.at[1 - slot], p_scr.at[slot]

    @pl.when(jnp.clip(s - 2, 0, n_blocks - 1) % n_e == 0)
    def _():
        acc_scr[...] = jnp.zeros_like(acc_scr)

    @pl.when(jnp.clip(s - 1, 0, n_blocks - 1) % n_e == 0)
    def _():
        for c in range(r2_scr.shape[1]):
            r2_scr[:, c] = r2_ref[:, :, c * LANES:(c + 1) * LANES]
            e2_scr[:, c] = e2_ref[:, :, c * LANES:(c + 1) * LANES]

    for cond, first, second in (((s >= 2) & (s < n_blocks), True, True),
                                (s < 2, True, False), (s >= n_blocks, False, True)):
        @pl.when(cond)
        def _(first=first, second=second):
            _peer_matmuls(h2t_ref, u_ref, vt_ref, a_w, p_r, acc_scr, first, second)

    @pl.when((s >= 1) & (s <= n_blocks))
    def _():
        _peer_weighting(r2_scr, e2_scr, n1_ref, e1z_ref, a_r, p_w)

    @pl.when((s >= 2) & ((s - 2) % n_e == n_e - 1))
    def _():
        gate2 = mod_ref[0, 5:6, :]
        peer = acc_scr[...].T
        y_ref[...] = _layer_norm(ALPHA * x1_ref[...] + gate2 * peer, g2_ref[...], b2_ref[...])


def _peer_call(h2t, r2, e2, n1, e1z, x1, mod, mod_row_of_tile, w):
    t = x1.shape[0]
    tt, et = PEER_TT, PEER_ET
    n_i = et // N_KEYS
    n_e = N_EXPERTS // et
    n_blocks = (t // tt) * n_e
    n_c = tt // LANES
    blk = lambda s, lag: jnp.clip(s - lag, 0, n_blocks - 1)
    tile = lambda s, lag: blk(s, lag) // n_e
    exp = lambda s, lag: blk(s, lag) % n_e
    rfull = pl.BlockSpec((PEER_HEADS, N_KEYS, tt), lambda s: (0, 0, tile(s, 1)))
    rrow = pl.BlockSpec((PEER_HEADS, n_i, tt), lambda s: (0, exp(s, 1), tile(s, 1)))
    return pl.pallas_call(
        functools.partial(_peer_kernel, n_e=n_e, n_blocks=n_blocks),
        grid=(n_blocks + 2,),
        in_specs=[pl.BlockSpec((D_MODEL, tt), lambda s: (0, tile(s, 0))),
                  rfull, rfull, rrow, rrow,
                  pl.BlockSpec((et, D_MODEL), lambda s: (exp(s, 0), 0)),
                  pl.BlockSpec((D_MODEL, et), lambda s: (0, exp(s, 2))),
                  pl.BlockSpec((tt, D_MODEL), lambda s: (tile(s, 2), 0)),
                  pl.BlockSpec((1, 6, D_MODEL), lambda s: (mod_row_of_tile(tile(s, 2)), 0, 0)),
                  pl.BlockSpec((1, D_MODEL), lambda s: (0, 0)),
                  pl.BlockSpec((1, D_MODEL), lambda s: (0, 0))],
        out_specs=pl.BlockSpec((tt, D_MODEL), lambda s: (tile(s, 2), 0)),
        out_shape=jax.ShapeDtypeStruct((t, D_MODEL), F32),
        scratch_shapes=[pltpu.VMEM((2, n_c, et, LANES), BF16), pltpu.VMEM((2, n_c, et, LANES), BF16),
                        pltpu.VMEM((D_MODEL, tt), F32),
                        pltpu.VMEM((PEER_HEADS, n_c, N_KEYS, LANES), BF16),
                        pltpu.VMEM((PEER_HEADS, n_c, N_KEYS, LANES), BF16)],
        compiler_params=_cparams(("arbitrary",)),
        name="peer_dense",
    )(h2t, r2, e2, n1, e1z, w["u_tab"], w["v_tabT"], x1, mod, w["ln2_g"], w["ln2_b"])


_SWAP64 = np.concatenate([np.arange(16, 32), np.arange(0, 16), np.arange(48, 64), np.arange(32, 48)])


def _rope_tables(n_pos):
    pos = np.arange(n_pos)
    row = (pos // GRID_W).astype(np.float32)
    col = (pos % GRID_W).astype(np.float32)
    f = QK_ROPE // 4
    freqs = (1.0 / (np.float32(ROPE_THETA) ** (np.arange(f, dtype=np.float32) / np.float32(f)))).astype(np.float32)
    ar = row[:, None] * freqs[None, :]
    ac = col[:, None] * freqs[None, :]
    zeros = np.zeros((n_pos, QK_ROPE), np.float32)
    cos_t = np.concatenate([np.cos(ar), np.cos(ar), np.cos(ac), np.cos(ac), zeros], axis=1)
    sin_t = np.concatenate([-np.sin(ar), np.sin(ar), -np.sin(ac), np.sin(ac), zeros], axis=1)
    return jnp.asarray(cos_t, F32), jnp.asarray(sin_t, F32)


def _identity_tables(n_pos):
    ones = np.ones((n_pos, QK_ROPE), np.float32)
    zeros = np.zeros((n_pos, QK_ROPE), np.float32)
    return jnp.asarray(np.concatenate([ones, zeros], axis=1)), jnp.zeros((n_pos, LANES), F32)


def _prep_weights(w_in, w_s, b_s, g_q, w_qb, g_kv, w_kvb, w_o, ln1_g, ln1_b, w_pq, sub_keys, u_tab, v_tab,
                  ln2_g, ln2_b):
    o_kr = 2 * A_WIDTH + Q_RANK + KV_RANK
    w_in_ext = jnp.concatenate([w_in, w_in[:, o_kr + _SWAP64]], axis=1).astype(BF16)
    wq = w_qb.reshape(Q_RANK, MLA_HEADS, QK_NOPE + QK_ROPE)
    w_qb_ext = jnp.concatenate([wq, wq[:, :, QK_NOPE + _SWAP64]], axis=2).reshape(Q_RANK, MLA_HEADS * HEAD_W)
    return {
        "w_in": w_in_ext,
        "w_s": w_s.astype(BF16),
        "bsb": jnp.broadcast_to(b_s[:, :, None], (A_GROUPS, CHUNK, LANES)),
        "g_q": g_q[None, :],
        "w_qb": w_qb_ext.astype(BF16),
        "g_kv": g_kv[None, :],
        "w_kvb": w_kvb.astype(BF16),
        "w_o": w_o.astype(BF16),
        "ln1_g": ln1_g[None, :],
        "ln1_b": ln1_b[None, :],
        "w_pqT": w_pq.T.astype(BF16),
        "sub_keys": sub_keys.reshape(PEER_HEADS * 2, N_KEYS, N_KEYS).astype(BF16),
        "u_tab": u_tab.astype(BF16),
        "v_tabT": v_tab.T.astype(BF16),
        "ln2_g": ln2_g[None, :],
        "ln2_b": ln2_b[None, :],
    }


def _trunk(x, mod, mod_row_of_seq, w, cos_t, sin_t, cache):
    b, n, _ = x.shape
    xf = x.reshape(b * n, D_MODEL)
    row_of = lambda tile_tokens: (lambda i: mod_row_of_seq(i // (n // tile_tokens)))
    a_out, q, k, v, ckv, kr = _pre_call(xf, mod, row_of(PRE_TT), n // PRE_TT, w, cos_t, sin_t)
    q = q.reshape(b, n, -1)
    k = k.reshape(b, n, -1)
    v = v.reshape(b, n, -1)
    if cache is not None:
        k = jnp.concatenate([cache[0], k], axis=1)
        v = jnp.concatenate([cache[1], v], axis=1)
    b_out = _attn_call(q, k, v).reshape(b * n, -1)
    x1, h2t, r2, e2, n1, e1z = _post_call(xf, a_out, b_out, mod, row_of(POST_TT), w)
    y = _peer_call(h2t, r2, e2, n1, e1z, x1, mod, row_of(PEER_TT), w)
    return y.reshape(b, n, D_MODEL), ckv, kr


def kernel(x_prompt, x_sample, cache_ckv, cache_krope, c, c_ctx, w_mod, b_mod, w_in, w_s, b_s, g_q, w_qb, g_kv,
           w_kvb, w_o, ln1_g, ln1_b, w_pq, sub_keys, u_tab, v_tab, ln2_g, ln2_b):
    batch, seq, _ = x_prompt.shape
    dec_batch, dec_seq, _ = x_sample.shape
    past = cache_ckv.shape[2]
    y_p, y_s = x_prompt, x_sample
    ckv_list, kr_list = [], []
    for l in range(DEPTH):
        w = _prep_weights(w_in[l], w_s[l], b_s[l], g_q[l], w_qb[l], g_kv[l], w_kvb[l], w_o[l], ln1_g[l], ln1_b[l],
                          w_pq[l], sub_keys[l], u_tab[l], v_tab[l], ln2_g[l], ln2_b[l])
        c_rows = jnp.concatenate([c_ctx[None, :], c, jnp.zeros((8 - 1 - dec_batch, D_MODEL), F32)], axis=0)
        mod = _mod_call(c_rows, w_mod[l], b_mod[l][None, :]).reshape(8, 6, D_MODEL)

        cos_c, sin_c = _identity_tables(seq)
        y_p, ckv_l, kr_l = _trunk(y_p, mod, lambda s: 0, w, cos_c, sin_c, None)
        ckv_list.append(ckv_l.reshape(batch, seq, KV_RANK))
        kr_list.append(kr_l.reshape(batch, seq, QK_ROPE))

        krp = jnp.pad(cache_krope[:, l].reshape(dec_batch * past, QK_ROPE), ((0, 0), (0, LANES - QK_ROPE)))
        k_c, v_c = _cache_call(cache_ckv[:, l].reshape(dec_batch * past, KV_RANK), krp, w["w_kvb"])
        cache = (k_c.reshape(dec_batch, past, -1), v_c.reshape(dec_batch, past, -1))
        cos_l, sin_l = _rope_tables(dec_seq)
        y_s, _, _ = _trunk(y_s, mod, lambda s: 1 + s, w, cos_l, sin_l, cache)
    new_ckv = jnp.stack(ckv_list, axis=1)
    new_krope = jnp.stack(kr_list, axis=1)
    return (y_p, y_s, new_ckv, new_krope)
```
